```python
import jax
import jax.numpy as jnp
from jax import lax
import numpy as np


D_MODEL = 1024
BATCH = 2
SEQ = 16384
DEPTH = 4

N_META = 16
N_MIXERS = 3
N_A = (DEPTH + 2) // 3
N_B = (DEPTH + 1) // 3
N_C = DEPTH // 3

DN_ALPHA = (2.0 * DEPTH) ** 0.25
DN_BETA = (8.0 * DEPTH) ** -0.25
LN_EPS = 1e-5
RMS_EPS = 1e-6

MLA_HEADS = 16
MLA_NOPE = 128
MLA_ROPE = 64
MLA_V = 128
MLA_Q_RANK = 256
MLA_KV_RANK = 128
MLA_IN = MLA_Q_RANK + MLA_KV_RANK + MLA_ROPE
ROPE_BASE = 10000.0
Q_BLOCK = 128

RW_HEAD = 64
RW_HEADS = D_MODEL // RW_HEAD
RW_DECAY_LORA = 64
RW_A_LORA = 64
RW_GATE_LORA = 160
RW_GN_EPS = 64e-5

LRU_WIDTH = 1280
LRU_BLOCKS = 5
LRU_BLOCK = LRU_WIDTH // LRU_BLOCKS
LRU_C = 8.0
LRU_CONV = 4
LRU_CONV_LEFT = 2

D_FF = 2816
FFN_CONV = 3
FFN_CONV_LEFT = 1

kernel_name = 'hybrid_mla_rwkv7_rglru_encoder'


def _layer_norm(x, g, b):
    xf = x.astype(jnp.float32)
    mu = jnp.mean(xf, axis=-1, keepdims=True)
    var = jnp.mean(jnp.square(xf - mu), axis=-1, keepdims=True)
    return ((xf - mu) * lax.rsqrt(var + LN_EPS)).astype(x.dtype) * g + b


def _rms_norm(x, g):
    xf = x.astype(jnp.float32)
    ms = jnp.mean(jnp.square(xf), axis=-1, keepdims=True)
    return (xf * lax.rsqrt(ms + RMS_EPS)).astype(x.dtype) * g


def _dw_conv(x, w, b, pad_left):
    k_width, c = w.shape
    y = lax.conv_general_dilated(
        x, w[:, None, :], window_strides=(1,),
        padding=[(pad_left, k_width - 1 - pad_left)],
        dimension_numbers=('NWC', 'WIO', 'NWC'), feature_group_count=c)
    return y + b


def _rope(x, cos, sin):
    x1, x2 = jnp.split(x, 2, axis=-1)
    return jnp.concatenate([x1 * cos - x2 * sin, x1 * sin + x2 * cos], axis=-1)


def _mla_mixer(x, cos, sin, w_in, q_norm, w_q_up, kv_norm, w_kv_up, w_o):
    B, T, _ = x.shape
    H = MLA_HEADS
    hproj = x @ w_in
    c_q, c_kv, k_pe = jnp.split(hproj, [MLA_Q_RANK, MLA_Q_RANK + MLA_KV_RANK], axis=-1)
    c_q = _rms_norm(c_q, q_norm)
    c_kv = _rms_norm(c_kv, kv_norm)
    q = (c_q @ w_q_up).reshape(B, T, H, MLA_NOPE + MLA_ROPE)
    q_nope, q_pe = jnp.split(q, [MLA_NOPE], axis=-1)
    q_pe = _rope(q_pe, cos[:, :, None, :], sin[:, :, None, :])
    k_pe = _rope(k_pe, cos, sin)
    w_uk = w_kv_up[:, :, :MLA_NOPE]
    w_uv = w_kv_up[:, :, MLA_NOPE:]
    q_lat = jnp.einsum('bthn,rhn->bthr', q_nope, w_uk)
    scale = (MLA_NOPE + MLA_ROPE) ** -0.5

    def attend(ql, qp):
        s = jnp.einsum('bqhr,bkr->bhqk', ql, c_kv) + jnp.einsum('bqhp,bkp->bhqk', qp, k_pe)
        p = jax.nn.softmax(s.astype(jnp.float32) * scale, axis=-1).astype(c_kv.dtype)
        return jnp.einsum('bhqk,bkr->bqhr', p, c_kv)

    o_meta = attend(q_lat[:, :N_META], q_pe[:, :N_META])
    n_blk = (T - N_META) // Q_BLOCK
    ql_b = jnp.moveaxis(q_lat[:, N_META:].reshape(B, n_blk, Q_BLOCK, H, MLA_KV_RANK), 1, 0)
    qp_b = jnp.moveaxis(q_pe[:, N_META:].reshape(B, n_blk, Q_BLOCK, H, MLA_ROPE), 1, 0)
    o_b = lax.map(lambda qs: attend(qs[0], qs[1]), (ql_b, qp_b))
    o_real = jnp.moveaxis(o_b, 0, 1).reshape(B, T - N_META, H, MLA_KV_RANK)
    o_lat = jnp.concatenate([o_meta, o_real], axis=1)
    o = jnp.einsum('bthr,rhv->bthv', o_lat, w_uv).reshape(B, T, H * MLA_V)
    return o @ w_o


def _wkv7_scan(r, w, a, b, k, v, reverse):
    B, T, H, N = r.shape

    def step(S, inp):
        r_t, w_t, a_t, b_t, k_t, v_t = inp
        sa = jnp.einsum('bhvk,bhk->bhv', S, a_t)
        S = S * w_t[:, :, None, :] + sa[..., None] * b_t[:, :, None, :] + v_t[..., None] * k_t[:, :, None, :]
        return S, jnp.einsum('bhvk,bhk->bhv', S, r_t)

    xs = tuple(jnp.swapaxes(t, 0, 1) for t in (r, w, a, b, k, v))
    S0 = jnp.zeros((B, H, N, N), jnp.float32)
    _, y = lax.scan(step, S0, xs, reverse=reverse)
    return jnp.swapaxes(y, 0, 1)


def _rwkv7_mixer(x, mu, w_rkv, w0, w1, w2, a0, a1, a2, g1, g2, k_k, k_a, r_k, gn_g, gn_b, w_o):
    B, T, D = x.shape
    H, N = RW_HEADS, RW_HEAD
    f32 = jnp.float32
    xp = jnp.pad(x, ((0, 0), (1, 1), (0, 0)))
    xx = 0.5 * (xp[:, :-2] + xp[:, 2:]) - x
    xr, xw, xk, xv, xa, xg = (x + xx * mu[i] for i in range(6))
    r = xr @ w_rkv[0]
    k = xk @ w_rkv[1]
    v = xv @ w_rkv[2]
    g = jax.nn.sigmoid(xg @ g1) @ g2

    def heads(t):
        return t.reshape(B, T, H, N).astype(f32)

    rf, kf, vf = heads(r), heads(k), heads(v)
    kk = heads(k * k_k)
    kk = kk * lax.rsqrt(jnp.maximum(jnp.sum(jnp.square(kk), axis=-1, keepdims=True), 1e-24))
    k_a_h = k_a.reshape(H, N).astype(f32)
    r_k_h = r_k.reshape(H, N).astype(f32)
    ys = []
    bonuses = []
    for d in range(2):
        w_log = -jax.nn.softplus(-(w0[d] + jnp.tanh(xw @ w1[d]) @ w2[d])) - 0.5
        decay = jnp.exp(-jnp.exp(heads(w_log)))
        a = heads(jax.nn.sigmoid(a0[d] + (xa @ a1[d]) @ a2[d]))
        kd = kf * (1.0 + (a - 1.0) * k_a_h)
        ys.append(_wkv7_scan(rf, decay, -kk, kk * a, kd, vf, reverse=(d == 1)))
        bonuses.append(jnp.sum(rf * kd * r_k_h, axis=-1, keepdims=True) * vf)
    y = ys[0] + ys[1]
    mu_y = jnp.mean(y, axis=-1, keepdims=True)
    var_y = jnp.mean(jnp.square(y - mu_y), axis=-1, keepdims=True)
    y = ((y - mu_y) * lax.rsqrt(var_y + RW_GN_EPS)).reshape(B, T, D) * gn_g + gn_b
    y = y + (bonuses[0] + bonuses[1]).reshape(B, T, D)
    return (y.astype(x.dtype) * g) @ w_o


def _linear_recurrence(a, b, reverse):
    def combine(lhs, rhs):
        a_l, b_l = lhs
        a_r, b_r = rhs
        return a_l * a_r, a_r * b_l + b_r
    _, h = lax.associative_scan(combine, (a, b), axis=1, reverse=reverse)
    return h


def _rglru_mixer(x, w_in, conv_w, conv_b, gate_w, gate_b, lam, w_o):
    B, T, _ = x.shape
    f32 = jnp.float32
    gate_branch, u = jnp.split(x @ w_in, 2, axis=-1)
    gate_branch = jax.nn.gelu(gate_branch, approximate=True)
    u = _dw_conv(u, conv_w, conv_b, LRU_CONV_LEFT)
    ub = u.reshape(B, T, LRU_BLOCKS, LRU_BLOCK)
    uf = u.astype(f32)
    hs = []
    for d in range(2):
        gates = jnp.einsum('btnj,gnjk->gbtnk', ub, gate_w[d]).reshape(2, B, T, LRU_WIDTH)
        gates = jax.nn.sigmoid((gates + gate_b[d][:, None, None, :]).astype(f32))
        r_gate, i_gate = gates[0], gates[1]
        log_a = -LRU_C * r_gate * jax.nn.softplus(-lam[d].astype(f32))
        a = jnp.exp(log_a)
        b = jnp.sqrt(-jnp.expm1(2.0 * log_a)) * (i_gate * uf)
        hs.append(_linear_recurrence(a, b, reverse=(d == 1)))
    h = (hs[0] + hs[1]).astype(x.dtype)
    return (h * gate_branch) @ w_o


def _conv_glu_ffn(x, w_in, conv_w, conv_b, w_out):
    g, u = jnp.split(x @ w_in, 2, axis=-1)
    g = _dw_conv(g, conv_w, conv_b, FFN_CONV_LEFT)
    return (jax.nn.silu(g) * u) @ w_out


def setup_inputs(seed: int = 0) -> dict:
    key = jax.random.key(seed)
    keys = iter(jax.random.split(key, 64))
    f32 = jnp.float32
    D = D_MODEL

    def nrm(shape, scale):
        return jax.random.normal(next(keys), shape, f32) * scale

    def gain(shape):
        return 1.0 + nrm(shape, 0.02)

    x = nrm((BATCH, SEQ, D), 1.0)
    offsets = jax.random.randint(next(keys), (BATCH, 1), 0, 4096, dtype=jnp.int32)
    positions = offsets + jnp.arange(SEQ, dtype=jnp.int32)[None, :]
    meta_tokens = nrm((N_META, D), 1.0)
    ln_g = gain((DEPTH, 2, D))
    ln_b = nrm((DEPTH, 2, D), 0.02)
    ffn_w_in = nrm((DEPTH, D, 2 * D_FF), D ** -0.5)
    ffn_conv_w = nrm((DEPTH, FFN_CONV, D_FF), FFN_CONV ** -0.5)
    ffn_conv_b = nrm((DEPTH, D_FF), 0.02)
    ffn_w_out = nrm((DEPTH, D_FF, D), DN_BETA * D_FF ** -0.5)
    mla_w_in = nrm((N_A, D, MLA_IN), D ** -0.5)
    mla_q_norm = gain((N_A, MLA_Q_RANK))
    mla_w_q_up = nrm((N_A, MLA_Q_RANK, MLA_HEADS * (MLA_NOPE + MLA_ROPE)), MLA_Q_RANK ** -0.5)
    mla_kv_norm = gain((N_A, MLA_KV_RANK))
    mla_w_kv_up = nrm((N_A, MLA_KV_RANK, MLA_HEADS, MLA_NOPE + MLA_V), MLA_KV_RANK ** -0.5)
    mla_w_o = nrm((N_A, MLA_HEADS * MLA_V, D), DN_BETA * (MLA_HEADS * MLA_V) ** -0.5)
    ratio = jnp.arange(D, dtype=f32) / (D - 1)
    rw_mu = jax.random.uniform(next(keys), (N_B, 6, D), f32)
    rw_w_rkv = nrm((N_B, 3, D, D), D ** -0.5)
    rw_w0 = (-6.5 + 5.0 * ratio ** 0.9) + nrm((N_B, 2, D), 0.1)
    rw_w1 = nrm((N_B, 2, D, RW_DECAY_LORA), D ** -0.5)
    rw_w2 = nrm((N_B, 2, RW_DECAY_LORA, D), 0.5 * RW_DECAY_LORA ** -0.5)
    rw_a0 = nrm((N_B, 2, D), 0.1)
    rw_a1 = nrm((N_B, 2, D, RW_A_LORA), D ** -0.5)
    rw_a2 = nrm((N_B, 2, RW_A_LORA, D), 0.5 * RW_A_LORA ** -0.5)
    rw_g1 = nrm((N_B, D, RW_GATE_LORA), D ** -0.5)
    rw_g2 = nrm((N_B, RW_GATE_LORA, D), RW_GATE_LORA ** -0.5)
    rw_k_k = 0.85 + nrm((N_B, D), 0.02)
    rw_k_a = 1.0 + nrm((N_B, D), 0.02)
    rw_r_k = nrm((N_B, D), 0.1)
    rw_gn_g = gain((N_B, D))
    rw_gn_b = nrm((N_B, D), 0.02)
    rw_w_o = nrm((N_B, D, D), DN_BETA * D ** -0.5)
    lru_w_in = nrm((N_C, D, 2 * LRU_WIDTH), D ** -0.5)
    lru_conv_w = nrm((N_C, LRU_CONV, LRU_WIDTH), LRU_CONV ** -0.5)
    lru_conv_b = nrm((N_C, LRU_WIDTH), 0.02)
    lru_gate_w = nrm((N_C, 2, 2, LRU_BLOCKS, LRU_BLOCK, LRU_BLOCK), LRU_BLOCK ** -0.5)
    lru_gate_b = nrm((N_C, 2, 2, LRU_WIDTH), 0.1)
    a_c = jax.random.uniform(next(keys), (N_C, 2, LRU_WIDTH), f32, 0.9, 0.999)
    a_base = a_c ** (1.0 / LRU_C)
    lru_lambda = jnp.log(a_base) - jnp.log1p(-a_base)
    lru_w_o = nrm((N_C, LRU_WIDTH, D), DN_BETA * LRU_WIDTH ** -0.5)
    return {
        'x': x, 'positions': positions, 'meta_tokens': meta_tokens,
        'ln_g': ln_g, 'ln_b': ln_b,
        'ffn_w_in': ffn_w_in, 'ffn_conv_w': ffn_conv_w, 'ffn_conv_b': ffn_conv_b, 'ffn_w_out': ffn_w_out,
        'mla_w_in': mla_w_in, 'mla_q_norm': mla_q_norm, 'mla_w_q_up': mla_w_q_up,
        'mla_kv_norm': mla_kv_norm, 'mla_w_kv_up': mla_w_kv_up, 'mla_w_o': mla_w_o,
        'rw_mu': rw_mu, 'rw_w_rkv': rw_w_rkv, 'rw_w0': rw_w0, 'rw_w1': rw_w1, 'rw_w2': rw_w2,
        'rw_a0': rw_a0, 'rw_a1': rw_a1, 'rw_a2': rw_a2, 'rw_g1': rw_g1, 'rw_g2': rw_g2,
        'rw_k_k': rw_k_k, 'rw_k_a': rw_k_a, 'rw_r_k': rw_r_k, 'rw_gn_g': rw_gn_g, 'rw_gn_b': rw_gn_b,
        'rw_w_o': rw_w_o,
        'lru_w_in': lru_w_in, 'lru_conv_w': lru_conv_w, 'lru_conv_b': lru_conv_b,
        'lru_gate_w': lru_gate_w, 'lru_gate_b': lru_gate_b, 'lru_lambda': lru_lambda, 'lru_w_o': lru_w_o,
    }


def reference(x, positions, meta_tokens, ln_g, ln_b,
              ffn_w_in, ffn_conv_w, ffn_conv_b, ffn_w_out,
              mla_w_in, mla_q_norm, mla_w_q_up, mla_kv_norm, mla_w_kv_up, mla_w_o,
              rw_mu, rw_w_rkv, rw_w0, rw_w1, rw_w2, rw_a0, rw_a1, rw_a2, rw_g1, rw_g2,
              rw_k_k, rw_k_a, rw_r_k, rw_gn_g, rw_gn_b, rw_w_o,
              lru_w_in, lru_conv_w, lru_conv_b, lru_gate_w, lru_gate_b, lru_lambda, lru_w_o):
    B = x.shape[0]
    dt = x.dtype
    h = jnp.concatenate([jnp.broadcast_to(meta_tokens[None].astype(dt), (B, N_META, D_MODEL)), x], axis=1)
    meta_pos = jnp.broadcast_to(jnp.arange(N_META, dtype=jnp.int32)[None, :], (B, N_META))
    pos = jnp.concatenate([meta_pos, positions + N_META], axis=1)
    inv_freq = ROPE_BASE ** (-jnp.arange(0, MLA_ROPE, 2, dtype=jnp.float32) / MLA_ROPE)
    ang = pos.astype(jnp.float32)[..., None] * inv_freq
    cos = jnp.cos(ang).astype(dt)
    sin = jnp.sin(ang).astype(dt)
    for i in range(DEPTH):
        kind = i % N_MIXERS
        j = i // N_MIXERS
        if kind == 0:
            m = _mla_mixer(h, cos, sin, mla_w_in[j], mla_q_norm[j], mla_w_q_up[j],
                           mla_kv_norm[j], mla_w_kv_up[j], mla_w_o[j])
        elif kind == 1:
            m = _rwkv7_mixer(h, rw_mu[j], rw_w_rkv[j], rw_w0[j], rw_w1[j], rw_w2[j],
                             rw_a0[j], rw_a1[j], rw_a2[j], rw_g1[j], rw_g2[j],
                             rw_k_k[j], rw_k_a[j], rw_r_k[j], rw_gn_g[j], rw_gn_b[j], rw_w_o[j])
        else:
            m = _rglru_mixer(h, lru_w_in[j], lru_conv_w[j], lru_conv_b[j], lru_gate_w[j],
                             lru_gate_b[j], lru_lambda[j], lru_w_o[j])
        h = _layer_norm(DN_ALPHA * h + m, ln_g[i, 0], ln_b[i, 0])
        f = _conv_glu_ffn(h, ffn_w_in[i], ffn_conv_w[i], ffn_conv_b[i], ffn_w_out[i])
        h = _layer_norm(DN_ALPHA * h + f, ln_g[i, 1], ln_b[i, 1])
    return h[:, N_META:]
```

```python
import functools
import math

import jax
import jax.numpy as jnp
from jax import lax
from jax.experimental import pallas as pl
from jax.experimental.pallas import tpu as pltpu

F32 = jnp.float32
BF16 = jnp.bfloat16

N_META = 16
LN_EPS = 1e-5
RMS_EPS = 1e-6
ROPE_BASE = 10000.0
MLA_NOPE = 128
MLA_ROPE = 64
MLA_V = 128
MLA_Q_RANK = 256
MLA_KV_RANK = 128
RW_HEAD = 64
RW_GN_EPS = 64e-5
LRU_C = 8.0
LRU_BLOCK = 256
SEQ_ALIGN = 256
HALO = 8
RW_CHUNK = 64
RW_GROUP = 4
MASK_NEG = -1e30
VMEM_LIMIT = 56 * 1024 * 1024


def _row_tile(tp, target):
    best = 128
    for t in range(128, min(tp, target) + 1, 128):
        if tp % t == 0:
            best = t
    return best


def _const_spec(shape):
    nd = len(shape)
    return pl.BlockSpec(shape, lambda *_: (0,) * nd, pipeline_mode=pl.Buffered(1))


def _row_spec(tm, c):
    return pl.BlockSpec((1, tm, c), lambda b, i: (b, i, 0))


def _halo_specs(tm, c, tp):
    nb = tm // HALO
    last = tp // HALO - 1
    prev = pl.BlockSpec((1, HALO, c), lambda b, i: (b, jnp.maximum(i * nb - 1, 0), 0))
    nxt = pl.BlockSpec((1, HALO, c), lambda b, i: (b, jnp.minimum((i + 1) * nb, last), 0))
    return prev, nxt


def _params(n_parallel=2):
    return pltpu.CompilerParams(
        dimension_semantics=("parallel",) * n_parallel,
        vmem_limit_bytes=VMEM_LIMIT)


def _dot(a, b):
    return jnp.dot(a, b, preferred_element_type=F32)


def _dot_nt(a, b):
    return lax.dot_general(a, b, (((1,), (1,)), ((), ())), preferred_element_type=F32)


def _dot_tn(a, b):
    return lax.dot_general(a, b, (((0,), (0,)), ((), ())), preferred_element_type=F32)


def _layer_norm(z, g, b):
    mu = jnp.mean(z, axis=-1, keepdims=True)
    zc = z - mu
    var = jnp.mean(zc * zc, axis=-1, keepdims=True)
    return zc * lax.rsqrt(var + LN_EPS) * g + b


def _sigmoid(z):
    return 1.0 / (1.0 + jnp.exp(-z))


def _softplus(z):
    return jnp.maximum(z, 0.0) + jnp.log(1.0 + jnp.exp(-jnp.abs(z)))


def _split2(z):
    hi = z.astype(BF16)
    lo = (z - hi.astype(F32)).astype(BF16)
    return hi, lo


def _ext_rows(x, prev, nxt, i, n_tiles, tm, pad):
    ext = jnp.concatenate([prev, x, nxt], axis=0)
    row = i * tm - HALO + lax.broadcasted_iota(jnp.int32, (tm + 2 * HALO, 1), 0)
    ok = (row >= pad) & (row < n_tiles * tm)
    return jnp.where(ok, ext, 0.0)


def _shift(ext, k, tm):
    n = ext.shape[0]
    if k == 0:
        return ext[HALO:HALO + tm]
    return pltpu.roll(ext, (-k) % n, 0)[HALO:HALO + tm]


def _mla_proj_kernel(x_ref, cos_ref, sin_ref, w_in_ref, qn_ref, kvn_ref, wq_ref, wuk_ref,
                     q_ref, k_ref, *, tm, pad, heads, qscale):
    i = pl.program_id(1)
    x = x_ref[0]
    hp = _dot(x.astype(BF16), w_in_ref[...])
    cq = hp[:, :MLA_Q_RANK]
    ckv = hp[:, MLA_Q_RANK:MLA_Q_RANK + MLA_KV_RANK]
    kpe = hp[:, 384:512]
    kpr = hp[:, 512:640]
    cq = cq * lax.rsqrt(jnp.mean(cq * cq, axis=-1, keepdims=True) + RMS_EPS) * qn_ref[...]
    ckv = ckv * lax.rsqrt(jnp.mean(ckv * ckv, axis=-1, keepdims=True) + RMS_EPS) * kvn_ref[...]
    cos = cos_ref[0]
    sin = sin_ref[0]
    lane = lax.broadcasted_iota(jnp.int32, (tm, 128), 1)
    row = i * tm + lax.broadcasted_iota(jnp.int32, (tm, 128), 0)
    kslab = kpe * cos + kpr * sin
    kslab = jnp.where(lane == MLA_ROPE, jnp.where(row < pad, MASK_NEG, 0.0), kslab)
    k_ref[0, :, 0:128] = ckv.astype(BF16)
    k_ref[0, :, 128:256] = kslab.astype(BF16)
    q = _dot(cq.astype(BF16), wq_ref[...])
    hw = heads * 128
    for h in range(heads):
        qn = q[:, h * 128:(h + 1) * 128]
        ql = _dot(qn.astype(BF16), wuk_ref[h])
        qp = q[:, hw + h * 128:hw + (h + 1) * 128] * cos + q[:, 2 * hw + h * 128:2 * hw + (h + 1) * 128] * sin
        qp = jnp.where(lane == MLA_ROPE, 1.0, qp * qscale)
        q_ref[0, h, :, 0:128] = (ql * qscale).astype(BF16)
        q_ref[0, h, :, 128:256] = qp.astype(BF16)


def _flash_kernel(q_ref, k_ref, o_ref, m_sc, l_sc, acc_sc, *, tq, tk, heads, n_kv):
    rows = heads * tq
    q = q_ref[0].reshape(rows, 256)
    m_sc[...] = jnp.full((rows, 1), -jnp.inf, F32)
    l_sc[...] = jnp.zeros((rows, 1), F32)
    acc_sc[...] = jnp.zeros((rows, 128), F32)

    def body(j, carry):
        start = pl.multiple_of(j * tk, tk)
        kc = k_ref[0, pl.ds(start, tk), :]
        s = _dot_nt(q, kc)
        m_old = m_sc[...]
        m_new = jnp.maximum(m_old, jnp.max(s, axis=-1, keepdims=True))
        alpha = jnp.exp2(m_old - m_new)
        p = jnp.exp2(s - m_new)
        l_sc[...] = alpha * l_sc[...] + jnp.sum(p, axis=-1, keepdims=True)
        acc_sc[...] = alpha * acc_sc[...] + _dot(p.astype(BF16), kc[:, 0:128])
        m_sc[...] = m_new
        return carry

    lax.fori_loop(0, n_kv, body, 0)
    out = acc_sc[...] / l_sc[...]
    for h in range(heads):
        o_ref[0, :, h * 128:(h + 1) * 128] = out[h * tq:(h + 1) * tq].astype(o_ref.dtype)


def _mla_out_kernel(ol_ref, x_ref, wuv_ref, wo_ref, g_ref, b_ref, o_ref, *, heads, alpha):
    ol = ol_ref[0]
    parts = [_dot(ol[:, h * 128:(h + 1) * 128], wuv_ref[h]).astype(BF16) for h in range(heads)]
    o = jnp.concatenate(parts, axis=-1)
    m = _dot(o, wo_ref[...])
    o_ref[0] = _layer_norm(alpha * x_ref[0] + m, g_ref[...], b_ref[...])


def _mla_layer(h, cos128, sin128, w_in, q_norm, w_q_up, kv_norm, w_kv_up, w_o, lg, lb, pad, alpha):
    B, Tp, D = h.shape
    heads = w_kv_up.shape[1]
    hw = heads * 128

    def rot(w):
        return jnp.concatenate([-w[..., MLA_ROPE // 2:], w[..., :MLA_ROPE // 2]], axis=-1)

    def slab(w):
        return jnp.concatenate([w, jnp.zeros(w.shape[:-1] + (128 - MLA_ROPE,), w.dtype)], axis=-1)

    kpe_w = w_in[:, MLA_Q_RANK + MLA_KV_RANK:]
    w_in_aug = jnp.concatenate(
        [w_in[:, :MLA_Q_RANK + MLA_KV_RANK], slab(kpe_w), slab(rot(kpe_w))], axis=1).astype(BF16)
    wq3 = w_q_up.reshape(MLA_Q_RANK, heads, MLA_NOPE + MLA_ROPE)
    wq_pe = wq3[:, :, MLA_NOPE:]
    wq_all = jnp.concatenate(
        [wq3[:, :, :MLA_NOPE].reshape(MLA_Q_RANK, hw),
         slab(wq_pe).reshape(MLA_Q_RANK, hw),
         slab(rot(wq_pe)).reshape(MLA_Q_RANK, hw)], axis=1).astype(BF16)
    w_ukT = jnp.transpose(w_kv_up[:, :, :MLA_NOPE], (1, 2, 0)).astype(BF16)
    w_uv = jnp.transpose(w_kv_up[:, :, MLA_NOPE:], (1, 0, 2)).astype(BF16)
    qscale = (MLA_NOPE + MLA_ROPE) ** -0.5 * math.log2(math.e)

    tm = _row_tile(Tp, 256)
    n = Tp // tm
    q, k = pl.pallas_call(
        functools.partial(_mla_proj_kernel, tm=tm, pad=pad, heads=heads, qscale=qscale),
        grid=(B, n),
        in_specs=[_row_spec(tm, D), _row_spec(tm, 128), _row_spec(tm, 128),
                  _const_spec(w_in_aug.shape), _const_spec((1, MLA_Q_RANK)), _const_spec((1, MLA_KV_RANK)),
                  _const_spec(wq_all.shape), _const_spec(w_ukT.shape)],
        out_specs=[pl.BlockSpec((1, heads, tm, 256), lambda b, i: (b, 0, i, 0)), _row_spec(tm, 256)],
        out_shape=[jax.ShapeDtypeStruct((B, heads, Tp, 256), BF16),
                   jax.ShapeDtypeStruct((B, Tp, 256), BF16)],
        compiler_params=_params(),
        name="mla_proj",
    )(h, cos128, sin128, w_in_aug, q_norm.reshape(1, -1), kv_norm.reshape(1, -1), wq_all, w_ukT)

    tq = 128
    tk = 256
    rows = heads * tq
    o_lat = pl.pallas_call(
        functools.partial(_flash_kernel, tq=tq, tk=tk, heads=heads, n_kv=Tp // tk),
        grid=(B, Tp // tq),
        in_specs=[pl.BlockSpec((1, heads, tq, 256), lambda b, i: (b, 0, i, 0)),
                  pl.BlockSpec((1, Tp, 256), lambda b, i: (b, 0, 0))],
        out_specs=_row_spec(tq, hw),
        out_shape=jax.ShapeDtypeStruct((B, Tp, hw), BF16),
        scratch_shapes=[pltpu.VMEM((rows, 1), F32), pltpu.VMEM((rows, 1), F32),
                        pltpu.VMEM((rows, 128), F32)],
        compiler_params=_params(),
        name="mla_flash",
    )(q, k)

    tm = _row_tile(Tp, 640)
    return pl.pallas_call(
        functools.partial(_mla_out_kernel, heads=heads, alpha=alpha),
        grid=(B, Tp // tm),
        in_specs=[_row_spec(tm, hw), _row_spec(tm, D), _const_spec(w_uv.shape),
                  _const_spec(w_o.shape), _const_spec((1, D)), _const_spec((1, D))],
        out_specs=_row_spec(tm, D),
        out_shape=jax.ShapeDtypeStruct((B, Tp, D), F32),
        compiler_params=_params(),
        name="mla_out",
    )(o_lat, h, w_uv, w_o.astype(BF16), lg.reshape(1, D), lb.reshape(1, D))


def _ffn_kernel(x_ref, xp_ref, xn_ref, wg_ref, wu_ref, cw_ref, cb_ref, wo_ref, g_ref, b_ref, o_ref,
                *, tm, pad, n_tiles, ff_chunk, alpha):
    i = pl.program_id(1)
    x = x_ref[0]
    xb = x.astype(BF16)
    ext = jnp.concatenate([xp_ref[0], x, xn_ref[0]], axis=0).astype(BF16)
    n_ext = tm + 2 * HALO
    row = i * tm - HALO + lax.broadcasted_iota(jnp.int32, (n_ext, 1), 0)
    ok = (row >= pad) & (row < n_tiles * tm)
    d_ff = wg_ref.shape[1]
    acc = jnp.zeros((tm, x.shape[1]), F32)
    for j in range(d_ff // ff_chunk):
        cs = slice(j * ff_chunk, (j + 1) * ff_chunk)
        g = jnp.where(ok, _dot(ext, wg_ref[:, cs]), 0.0)
        u = _dot(xb, wu_ref[:, cs])
        gc = (cw_ref[0:1, cs] * _shift(g, -1, tm) + cw_ref[1:2, cs] * _shift(g, 0, tm)
              + cw_ref[2:3, cs] * _shift(g, 1, tm) + cb_ref[:, cs])
        act = gc * _sigmoid(gc) * u
        acc = acc + _dot(act.astype(BF16), wo_ref[cs, :])
    o_ref[0] = _layer_norm(alpha * x + acc, g_ref[...], b_ref[...])


def _ffn_layer(h, w_in, conv_w, conv_b, w_out, lg, lb, pad, alpha):
    B, Tp, D = h.shape
    d_ff = w_out.shape[0]
    tm = _row_tile(Tp, 640)
    n = Tp // tm
    prev, nxt = _halo_specs(tm, D, Tp)
    wg = w_in[:, :d_ff].astype(BF16)
    wu = w_in[:, d_ff:].astype(BF16)
    return pl.pallas_call(
        functools.partial(_ffn_kernel, tm=tm, pad=pad, n_tiles=n, ff_chunk=256, alpha=alpha),
        grid=(B, n),
        in_specs=[_row_spec(tm, D), prev, nxt, _const_spec(wg.shape), _const_spec(wu.shape),
                  _const_spec(conv_w.shape), _const_spec((1, d_ff)), _const_spec(w_out.shape),
                  _const_spec((1, D)), _const_spec((1, D))],
        out_specs=_row_spec(tm, D),
        out_shape=jax.ShapeDtypeStruct((B, Tp, D), F32),
        compiler_params=_params(),
        name="ffn",
    )(h, h, h, wg, wu, conv_w, conv_b.reshape(1, d_ff), w_out.astype(BF16),
      lg.reshape(1, D), lb.reshape(1, D))


def _seg_sum(z, e_ref):
    hi, lo = _split2(z)
    return _dot(hi, e_ref[...]) + _dot(lo, e_ref[...])


def _seg_bcast(zs, et_ref):
    hi, lo = _split2(zs)
    return _dot(hi, et_ref[...]) + _dot(lo, et_ref[...])


def _rw_proj_kernel(x_ref, xp_ref, xn_ref, mu_ref, wr_ref, wk_ref, wv_ref, g1_ref, g2_ref,
                    w1_ref, w2_ref, w0_ref, a1_ref, a2_ref, a0_ref, kk_ref, ka_ref, rk_ref,
                    e_ref, et_ref,
                    r_out, v_out, kk_out, g_out, bonus_out, lw0_out, lw1_out, b0_out, b1_out,
                    kd0_out, kd1_out, *, tm, pad, n_tiles):
    i = pl.program_id(1)
    ext = _ext_rows(x_ref[0], xp_ref[0], xn_ref[0], i, n_tiles, tm, pad)
    x = _shift(ext, 0, tm)
    xx = 0.5 * (_shift(ext, -1, tm) + _shift(ext, 1, tm)) - x
    mix = [(x + xx * mu_ref[c:c + 1, :]).astype(BF16) for c in range(6)]
    xr, xw, xk, xv, xa, xg = mix
    row = i * tm + lax.broadcasted_iota(jnp.int32, (tm, 1), 0)
    ok = row >= pad
    r = _dot(xr, wr_ref[...])
    k = jnp.where(ok, _dot(xk, wk_ref[...]), 0.0)
    v = jnp.where(ok, _dot(xv, wv_ref[...]), 0.0)
    g = _dot(_sigmoid(_dot(xg, g1_ref[...])).astype(BF16), g2_ref[...])
    tw = jnp.tanh(_dot(xw, w1_ref[...])).astype(BF16)
    ta = _dot(xa, a1_ref[...]).astype(BF16)
    kk = k * kk_ref[...]
    ss = _seg_sum(kk * kk, e_ref)
    kk = kk * _seg_bcast(lax.rsqrt(jnp.maximum(ss, 1e-24)), et_ref)
    r_out[0] = r
    v_out[0] = v
    kk_out[0] = kk
    g_out[0] = g
    kd_sum = None
    for d, (lw_out, b_out, kd_out) in enumerate(((lw0_out, b0_out, kd0_out), (lw1_out, b1_out, kd1_out))):
        wl = w0_ref[d:d + 1, :] + _dot(tw[:, d * 128:(d + 1) * 128], w2_ref[d])
        w_log = -_softplus(-wl) - 0.5
        lw_out[0] = -jnp.exp(w_log)
        a = _sigmoid(a0_ref[d:d + 1, :] + _dot(ta[:, d * 128:(d + 1) * 128], a2_ref[d]))
        kd = k * (1.0 + (a - 1.0) * ka_ref[...])
        kd_out[0] = kd
        b_out[0] = kk * a
        kd_sum = kd if kd_sum is None else kd_sum + kd
    bs = _seg_sum(r * kd_sum * rk_ref[...], e_ref)
    bonus_out[0] = _seg_bcast(bs, et_ref) * v


def _rw_scan_kernel(r_ref, lw_ref, kk_ref, b_ref, kd_ref, v_ref, y_ref, h_sc, *, rev, n_sub):
    C = RW_CHUNK
    W = RW_GROUP * RW_HEAD
    groups = r_ref.shape[2] // W

    @pl.when(pl.program_id(1) == 0)
    def _():
        h_sc[...] = jnp.zeros_like(h_sc)

    ti = lax.broadcasted_iota(jnp.int32, (C, W), 0)
    si = lax.broadcasted_iota(jnp.int32, (C, W), 1) % RW_HEAD
    if rev:
        before = si > ti
    else:
        before = si < ti
    incl = before | (si == ti)
    eye = (si == ti).astype(F32)
    lvl_masks = []
    m = 1
    while m < C:
        same = (ti // (2 * m)) == (si // (2 * m))
        t_hi = (ti // m) % 2 == 1
        s_hi = (si // m) % 2 == 1
        if rev:
            lvl_masks.append(same & (~t_hi) & s_hi)
        else:
            lvl_masks.append(same & t_hi & (~s_hi))
        m *= 2
    bi = lax.broadcasted_iota(jnp.int32, (W, W), 0) // RW_HEAD
    bj = lax.broadcasted_iota(jnp.int32, (W, W), 1) // RW_HEAD
    bd_mask = bi == bj
    ones_bd = bd_mask.astype(BF16)
    ci =lax.broadcasted_iota(jnp.int32, (C, C), 0)
    cj = lax.broadcasted_iota(jnp.int32, (C, C), 1)
    tri = ((cj >= ci) if rev else (cj <= ci)).astype(BF16)

    def bd(z):
        zb = z.astype(BF16)
        zt = jnp.concatenate([zb] * RW_GROUP, axis=0)
        return jnp.where(bd_mask, zt, jnp.zeros_like(zt))

    def fold(z):
        zm = jnp.where(bd_mask, z, 0.0)
        out = zm[0:RW_HEAD]
        for a in range(1, RW_GROUP):
            out = out + zm[a * RW_HEAD:(a + 1) * RW_HEAD]
        return out

    def chunk(sub, carry):
        c_idx = (n_sub - 1 - sub) if rev else sub
        r0 = pl.multiple_of(c_idx * C, C)
        for gi in range(groups):
            ls = slice(gi * W, (gi + 1) * W)
            rr = r_ref[0, pl.ds(r0, C), ls]
            lw = lw_ref[0, pl.ds(r0, C), ls]
            kk = kk_ref[0, pl.ds(r0, C), ls]
            bb = b_ref[0, pl.ds(r0, C), ls]
            kd = kd_ref[0, pl.ds(r0, C), ls]
            vv = v_ref[0, pl.ds(r0, C), ls]
            H = h_sc[gi]
            l1 = lw.astype(BF16)
            rem = lw - l1.astype(F32)
            l2 = rem.astype(BF16)
            l3 = (rem - l2.astype(F32)).astype(BF16)
            gsum = _dot(tri, l1) + _dot(tri, l2) + _dot(tri, l3)
            gtot = gsum[0:1] if rev else gsum[C - 1:C]
            eg = jnp.exp(gsum)
            en = jnp.exp(-gsum)
            ec = jnp.exp(gtot - gsum)
            At = -kk * jnp.exp(gsum - lw)
            Rt = rr * eg
            Bt = bb * en
            Kt = kd * en
            lhs = jnp.concatenate([At, Rt], axis=0).astype(BF16)
            sab = _dot_nt(lhs, bd(Bt))
            sak = _dot_nt(lhs, bd(Kt))
            A_ab = sab[0:C]
            A_ak = jnp.where(before, sak[0:C], 0.0)
            A_rb = jnp.where(incl, sab[C:2 * C], 0.0)
            A_rk = jnp.where(incl, sak[C:2 * C], 0.0)
            X = eye + jnp.where(lvl_masks[0], A_ab, 0.0)
            for lm in lvl_masks[1:]:
                P = _dot(X.astype(BF16), bd(jnp.where(lm, A_ab, 0.0)))
                X = X + _dot(P.astype(BF16), bd(X))
            Hb = bd(H)
            Vb = bd(vv)
            Wm = _dot(At.astype(BF16), Hb) + _dot(A_ak.astype(BF16), Vb)
            U = _dot(X.astype(BF16), bd(Wm))
            Y = _dot(Rt.astype(BF16), Hb) + _dot(A_rb.astype(BF16), bd(U)) + _dot(A_rk.astype(BF16), Vb)
            y_ref[0, pl.ds(r0, C), ls] = Y
            gd = eye * gtot
            d1 = gd.astype(BF16)
            drem = gd - d1.astype(F32)
            d2 = drem.astype(BF16)
            d3 = (drem - d2.astype(F32)).astype(BF16)
            gam = jnp.exp(_dot(d1, ones_bd) + _dot(d2, ones_bd) + _dot(d3, ones_bd))
            lhs_t = jnp.concatenate([bb * ec, kd * ec], axis=0).astype(BF16)
            rhs_t = jnp.concatenate([U, vv], axis=0).astype(BF16)
            h_sc[gi] = gam * H + fold(_dot_tn(lhs_t, rhs_t))
        return carry

    lax.fori_loop(0, n_sub, chunk, 0)


def _rw_out_kernel(y0_ref, y1_ref, bonus_ref, g_ref, x_ref, gng_ref, gnb_ref, wo_ref, e_ref, et_ref,
                   lg_ref, lb_ref, o_ref, *, alpha):
    y = y0_ref[0] + y1_ref[0]
    inv_n = 1.0 / RW_HEAD
    mu = _seg_bcast(_seg_sum(y, e_ref) * inv_n, et_ref)
    yc = y - mu
    var = _seg_sum(yc * yc, e_ref) * inv_n
    yn = yc * _seg_bcast(lax.rsqrt(var + RW_GN_EPS), et_ref) * gng_ref[...] + gnb_ref[...]
    yo = (yn + bonus_ref[0]) * g_ref[0]
    m = _dot(yo.astype(BF16), wo_ref[...])
    o_ref[0] = _layer_norm(alpha * x_ref[0] + m, lg_ref[...], lb_ref[...])


def _rwkv_layer(h, mu, w_rkv, w0, w1, w2, a0, a1, a2, g1, g2, k_k, k_a, r_k, gn_g, gn_b, w_o,
                lg, lb, pad, alpha):
    B, Tp, D = h.shape
    heads = D // RW_HEAD

    def pad_cols(w, n):
        return jnp.concatenate([w, jnp.zeros(w.shape[:-1] + (n - w.shape[-1],), w.dtype)], axis=-1)

    def pad_rows(w, n):
        return jnp.concatenate([w, jnp.zeros(w.shape[:-2] + (n - w.shape[-2], w.shape[-1]), w.dtype)], axis=-2)

    g1p = pad_cols(g1, 256).astype(BF16)
    g2p = pad_rows(g2, 256).astype(BF16)
    w1p = jnp.concatenate([pad_cols(w1[0], 128), pad_cols(w1[1], 128)], axis=1).astype(BF16)
    a1p = jnp.concatenate([pad_cols(a1[0], 128), pad_cols(a1[1], 128)], axis=1).astype(BF16)
    w2p = pad_rows(w2, 128).astype(BF16)
    a2p = pad_rows(a2, 128).astype(BF16)
    e = (jnp.arange(D)[:, None] // RW_HEAD == jnp.arange(128)[None, :]).astype(BF16)
    et = e.T

    tm = _row_tile(Tp, 256)
    n = Tp // tm
    prev, nxt = _halo_specs(tm, D, Tp)
    vec = _const_spec((1, D))
    outs = pl.pallas_call(
        functools.partial(_rw_proj_kernel, tm=tm, pad=pad, n_tiles=n),
        grid=(B, n),
        in_specs=[_row_spec(tm, D), prev, nxt, _const_spec((6, D)),
                  _const_spec((D, D)), _const_spec((D, D)), _const_spec((D, D)),
                  _const_spec(g1p.shape), _const_spec(g2p.shape),
                  _const_spec(w1p.shape), _const_spec(w2p.shape), _const_spec((2, D)),
                  _const_spec(a1p.shape), _const_spec(a2p.shape), _const_spec((2, D)),
                  vec, vec, vec, _const_spec(e.shape), _const_spec(et.shape)],
        out_specs=[_row_spec(tm, D)] * 11,
        out_shape=[jax.ShapeDtypeStruct((B, Tp, D), F32)] * 11,
        compiler_params=_params(),
        name="rw_proj",
    )(h, h, h, mu, w_rkv[0].astype(BF16), w_rkv[1].astype(BF16), w_rkv[2].astype(BF16), g1p, g2p,
      w1p, w2p, w0, a1p, a2p, a0, k_k.reshape(1, D), k_a.reshape(1, D), r_k.reshape(1, D), e, et)
    r, v, kk, g, bonus, lw0, lw1, b0, b1, kd0, kd1 = outs

    ts = _row_tile(Tp, 256)
    ns = Tp // ts
    ys = []
    for rev, lw, bb, kd in ((False, lw0, b0, kd0), (True, lw1, b1, kd1)):
        if rev:
            spec = pl.BlockSpec((1, ts, D), lambda b, i: (b, ns - 1 - i, 0))
        else:
            spec = _row_spec(ts, D)
        ys.append(pl.pallas_call(
            functools.partial(_rw_scan_kernel, rev=rev, n_sub=ts // RW_CHUNK),
            grid=(B, ns),
            in_specs=[spec] * 6,
            out_specs=spec,
            out_shape=jax.ShapeDtypeStruct((B, Tp, D), F32),
            scratch_shapes=[pltpu.VMEM((D // (RW_GROUP * RW_HEAD), RW_HEAD, RW_GROUP * RW_HEAD), F32)],
            compiler_params=pltpu.CompilerParams(
                dimension_semantics=("parallel", "arbitrary"), vmem_limit_bytes=VMEM_LIMIT),
            name="rw_scan_bwd" if rev else "rw_scan_fwd",
        )(r, lw, kk, bb, kd, v))

    tm = _row_tile(Tp, 640)
    return pl.pallas_call(
        functools.partial(_rw_out_kernel, alpha=alpha),
        grid=(B, Tp // tm),
        in_specs=[_row_spec(tm, D)] * 5 + [vec, vec, _const_spec((D, D)), _const_spec(e.shape),
                                            _const_spec(et.shape), vec, vec],
        out_specs=_row_spec(tm, D),
        out_shape=jax.ShapeDtypeStruct((B, Tp, D), F32),
        compiler_params=_params(),
        name="rw_out",
    )(ys[0], ys[1], bonus, g, h, gn_g.reshape(1, D), gn_b.reshape(1, D), w_o.astype(BF16), e, et,
      lg.reshape(1, D), lb.reshape(1, D))


def _gelu_tanh(z):
    return 0.5 * z * (1.0 + jnp.tanh(math.sqrt(2.0 / math.pi) * (z + 0.044715 * z * z * z)))


def _lru_proj_kernel(x_ref, wg_ref, wu_ref, gate_out, u_out, *, tm, pad):
    i = pl.program_id(1)
    xb = x_ref[0].astype(BF16)
    gate_out[0] = _gelu_tanh(_dot(xb, wg_ref[...]))
    row = i * tm + lax.broadcasted_iota(jnp.int32, (tm, 1), 0)
    u_out[0] = jnp.where(row >= pad, _dot(xb, wu_ref[...]), 0.0)


def _lru_scan_kernel(u_ref, up_ref, un_ref, cw_ref, cb_ref, gw_ref, gb_ref, lam_ref, h_out,
                     a_sc, b_sc, h_sc, *, tm, pad, n_tiles, rev):
    step = pl.program_id(1)
    i = (n_tiles - 1 - step) if rev else step

    @pl.when(step == 0)
    def _():
        h_sc[...] = jnp.zeros_like(h_sc)

    ext = _ext_rows(u_ref[0], up_ref[0], un_ref[0], i, n_tiles, tm, 0)
    uc = cb_ref[...]
    for kt in range(cw_ref.shape[0]):
        uc = uc + cw_ref[kt:kt + 1, :] * _shift(ext, kt - 2, tm)
    width = uc.shape[1]
    nblk = width // LRU_BLOCK
    sp = _softplus(-lam_ref[...])
    row = i * tm + lax.broadcasted_iota(jnp.int32, (tm, 1), 0)
    ok = row >= pad
    for nb in range(nblk):
        cs = slice(nb * LRU_BLOCK, (nb + 1) * LRU_BLOCK)
        ub = uc[:, cs]
        ubb = ub.astype(BF16)
        rg = _sigmoid(_dot(ubb, gw_ref[0, nb]) + gb_ref[0:1, cs])
        ig = _sigmoid(_dot(ubb, gw_ref[1, nb]) + gb_ref[1:2, cs])
        a = jnp.exp(-LRU_C * rg * sp[:, cs])
        a_sc[:, cs] = a
        b_sc[:, cs] = jnp.where(ok, jnp.sqrt(1.0 - a * a) * (ig * ub), 0.0)

    def body(s, hcur):
        t = (tm - 1 - s) if rev else s
        hnew = a_sc[pl.ds(t, 1), :] * hcur + b_sc[pl.ds(t, 1), :]
        h_out[0, pl.ds(t, 1), :] = hnew
        return hnew

    h_sc[...] = lax.fori_loop(0, tm, body, h_sc[...], unroll=8)


def _lru_out_kernel(h0_ref, h1_ref, gate_ref, x_ref, wo_ref, lg_ref, lb_ref, o_ref, *, alpha):
    hh = (h0_ref[0] + h1_ref[0]) * gate_ref[0]
    m = _dot(hh.astype(BF16), wo_ref[...])
    o_ref[0] = _layer_norm(alpha * x_ref[0] + m, lg_ref[...], lb_ref[...])


def _lru_layer(h, w_in, conv_w, conv_b, gate_w, gate_b, lam, w_o, lg, lb, pad, alpha):
    B, Tp, D = h.shape
    width = w_o.shape[0]
    tm = _row_tile(Tp, 640)
    n = Tp // tm
    gate, u = pl.pallas_call(
        functools.partial(_lru_proj_kernel, tm=tm, pad=pad),
        grid=(B, n),
        in_specs=[_row_spec(tm, D), _const_spec((D, width)), _const_spec((D, width))],
        out_specs=[_row_spec(tm, width)] * 2,
        out_shape=[jax.ShapeDtypeStruct((B, Tp, width), F32)] * 2,
        compiler_params=_params(),
        name="lru_proj",
    )(h, w_in[:, :width].astype(BF16), w_in[:, width:].astype(BF16))

    ts = _row_tile(Tp, 256)
    ns = Tp // ts
    nb8 = ts // HALO
    last = Tp // HALO - 1
    hs = []
    for d, rev in ((0, False), (1, True)):
        if rev:
            cur = pl.BlockSpec((1, ts, width), lambda b, s: (b, ns - 1 - s, 0))
            prev = pl.BlockSpec((1, HALO, width), lambda b, s: (b, jnp.maximum((ns - 1 - s) * nb8 - 1, 0), 0))
            nxt = pl.BlockSpec((1, HALO, width), lambda b, s: (b, jnp.minimum((ns - s) * nb8, last), 0))
        else:
            cur = _row_spec(ts, width)
            prev, nxt = _halo_specs(ts, width, Tp)
        hs.append(pl.pallas_call(
            functools.partial(_lru_scan_kernel, tm=ts, pad=pad, n_tiles=ns, rev=rev),
            grid=(B, ns),
            in_specs=[cur, prev, nxt, _const_spec(conv_w.shape), _const_spec((1, width)),
                      _const_spec(gate_w.shape[1:]), _const_spec((2, width)), _const_spec((1, width))],
            out_specs=cur,
            out_shape=jax.ShapeDtypeStruct((B, Tp, width), F32),
            scratch_shapes=[pltpu.VMEM((ts, width), F32), pltpu.VMEM((ts, width), F32),
                            pltpu.VMEM((1, width), F32)],
            compiler_params=pltpu.CompilerParams(
                dimension_semantics=("parallel", "arbitrary"), vmem_limit_bytes=VMEM_LIMIT),
            name="lru_scan_bwd" if rev else "lru_scan_fwd",
        )(u, u, u, conv_w, conv_b.reshape(1, width), gate_w[d].astype(BF16), gate_b[d],
          lam[d].reshape(1, width)))

    return pl.pallas_call(
        functools.partial(_lru_out_kernel, alpha=alpha),
        grid=(B, n),
        in_specs=[_row_spec(tm, width)] * 3 + [_row_spec(tm, D), _const_spec((width, D)),
                                               _const_spec((1, D)), _const_spec((1, D))],
        out_specs=_row_spec(tm, D),
        out_shape=jax.ShapeDtypeStruct((B, Tp, D), F32),
        compiler_params=_params(),
        name="lru_out",
    )(hs[0], hs[1], gate, h, w_o.astype(BF16), lg.reshape(1, D), lb.reshape(1, D))


def kernel(x, positions, meta_tokens, ln_g, ln_b, ffn_w_in, ffn_conv_w, ffn_conv_b, ffn_w_out, mla_w_in, mla_q_norm, mla_w_q_up, mla_kv_norm, mla_w_kv_up, mla_w_o, rw_mu, rw_w_rkv, rw_w0, rw_w1, rw_w2, rw_a0, rw_a1, rw_a2, rw_g1, rw_g2, rw_k_k, rw_k_a, rw_r_k, rw_gn_g, rw_gn_b, rw_w_o, lru_w_in, lru_conv_w, lru_conv_b, lru_gate_w, lru_gate_b, lru_lambda, lru_w_o):
    B, S, D = x.shape
    depth = ln_g.shape[0]
    T = S + N_META
    Tp = -(-T // SEQ_ALIGN) * SEQ_ALIGN
    pad = Tp - T
    alpha = (2.0 * depth) ** 0.25
    dt = x.dtype
    h = jnp.concatenate([jnp.zeros((B, pad, D), dt),
                         jnp.broadcast_to(meta_tokens[None].astype(dt), (B, N_META, D)), x], axis=1)
    pos = jnp.concatenate(
        [jnp.zeros((B, pad), jnp.int32),
         jnp.broadcast_to(jnp.arange(N_META, dtype=jnp.int32)[None, :], (B, N_META)),
         positions + N_META], axis=1)
    inv_freq = ROPE_BASE ** (-jnp.arange(0, MLA_ROPE, 2, dtype=F32) / MLA_ROPE)
    ang = pos.astype(F32)[..., None] * inv_freq
    zeros = jnp.zeros((B, Tp, 128 - MLA_ROPE), F32)
    cos = jnp.cos(ang)
    sin = jnp.sin(ang)
    cos128 = jnp.concatenate([cos, cos, zeros], axis=-1)
    sin128 = jnp.concatenate([sin, sin, zeros], axis=-1)
    for i in range(depth):
        kind = i % 3
        j = i // 3
        lg, lb = ln_g[i, 0], ln_b[i, 0]
        if kind == 0:
            h = _mla_layer(h, cos128, sin128, mla_w_in[j], mla_q_norm[j], mla_w_q_up[j], mla_kv_norm[j],
                           mla_w_kv_up[j], mla_w_o[j], lg, lb, pad, alpha)
        elif kind == 1:
            h = _rwkv_layer(h, rw_mu[j], rw_w_rkv[j], rw_w0[j], rw_w1[j], rw_w2[j], rw_a0[j], rw_a1[j],
                            rw_a2[j], rw_g1[j], rw_g2[j], rw_k_k[j], rw_k_a[j], rw_r_k[j], rw_gn_g[j],
                            rw_gn_b[j], rw_w_o[j], lg, lb, pad, alpha)
        else:
            h = _lru_layer(h, lru_w_in[j], lru_conv_w[j], lru_conv_b[j], lru_gate_w[j], lru_gate_b[j],
                           lru_lambda[j], lru_w_o[j], lg, lb, pad, alpha)
        h = _ffn_layer(h, ffn_w_in[i], ffn_conv_w[i], ffn_conv_b[i], ffn_w_out[i],
                       ln_g[i, 1], ln_b[i, 1], pad, alpha)
    return h[:, pad + N_META:]
```

```python
import functools
import math

import jax
import jax.numpy as jnp
from jax import lax
from jax.experimental import pallas as pl
from jax.experimental.pallas import tpu as pltpu

F32 = jnp.float32
BF16 = jnp.bfloat16

N_META = 16
LN_EPS = 1e-5
RMS_EPS = 1e-6
ROPE_BASE = 10000.0
MLA_NOPE = 128
MLA_ROPE = 64
MLA_V = 128
MLA_Q_RANK = 256
MLA_KV_RANK = 128
RW_HEAD = 64
RW_GN_EPS = 64e-5
LRU_C = 8.0
LRU_BLOCK = 256
SEQ_ALIGN = 256
HALO = 8
RW_CHUNK = 64
RW_GROUP = 4
MASK_NEG = -1e30
MASK_LANE = 128 + MLA_ROPE
ONES_LANE = MASK_LANE + 1
VMEM_LIMIT = 56 * 1024 * 1024


def _row_tile(tp, target):
    best = 128
    for t in range(128, min(tp, target) + 1, 128):
        if tp % t == 0:
            best = t
    return best


def _const_spec(shape):
    nd = len(shape)
    return pl.BlockSpec(shape, lambda *_: (0,) * nd, pipeline_mode=pl.Buffered(1))


def _row_spec(tm, c):
    return pl.BlockSpec((1, tm, c), lambda b, i: (b, i, 0))


def _halo_specs(tm, c, tp):
    nb = tm // HALO
    last = tp // HALO - 1
    prev = pl.BlockSpec((1, HALO, c), lambda b, i: (b, jnp.maximum(i * nb - 1, 0), 0))
    nxt = pl.BlockSpec((1, HALO, c), lambda b, i: (b, jnp.minimum((i + 1) * nb, last), 0))
    return prev, nxt


def _params(n_parallel=2):
    return pltpu.CompilerParams(
        dimension_semantics=("parallel",) * n_parallel,
        vmem_limit_bytes=VMEM_LIMIT)


def _dot(a, b):
    return jnp.dot(a, b, preferred_element_type=F32)


def _dot_nt(a, b):
    return lax.dot_general(a, b, (((1,), (1,)), ((), ())), preferred_element_type=F32)


def _dot_tn(a, b):
    return lax.dot_general(a, b, (((0,), (0,)), ((), ())), preferred_element_type=F32)


def _layer_norm(z, g, b):
    mu = jnp.mean(z, axis=-1, keepdims=True)
    zc = z - mu
    var = jnp.mean(zc * zc, axis=-1, keepdims=True)
    return zc * lax.rsqrt(var + LN_EPS) * g + b


def _sigmoid(z):
    return 1.0 / (1.0 + jnp.exp(-z))


def _softplus(z):
    return jnp.maximum(z, 0.0) + jnp.log(1.0 + jnp.exp(-jnp.abs(z)))


def _split2(z):
    hi = z.astype(BF16)
    lo = (z - hi.astype(F32)).astype(BF16)
    return hi, lo


def _ext_rows(x, prev, nxt, i, n_tiles, tm, pad):
    ext = jnp.concatenate([prev, x, nxt], axis=0)
    row = i * tm - HALO + lax.broadcasted_iota(jnp.int32, (tm + 2 * HALO, 1), 0)
    ok = (row >= pad) & (row < n_tiles * tm)
    return jnp.where(ok, ext, 0.0)


def _shift(ext, k, tm):
    n = ext.shape[0]
    if k == 0:
        return ext[HALO:HALO + tm]
    return pltpu.roll(ext, (-k) % n, 0)[HALO:HALO + tm]


def _mla_proj_kernel(x_ref, cos_ref, sin_ref, w_in_ref, qn_ref, kvn_ref, wq_ref, wuk_ref,
                     q_ref, k_ref, *, tm, pad, heads, qscale):
    i = pl.program_id(1)
    x = x_ref[0]
    hp = _dot(x.astype(BF16), w_in_ref[...])
    cq = hp[:, :MLA_Q_RANK]
    ckv = hp[:, MLA_Q_RANK:MLA_Q_RANK + MLA_KV_RANK]
    kpe = hp[:, 384:512]
    kpr = hp[:, 512:640]
    cq = cq * lax.rsqrt(jnp.mean(cq * cq, axis=-1, keepdims=True) + RMS_EPS) * qn_ref[...]
    ckv = ckv * lax.rsqrt(jnp.mean(ckv * ckv, axis=-1, keepdims=True) + RMS_EPS) * kvn_ref[...]
    cos = cos_ref[0]
    sin = sin_ref[0]
    lane = lax.broadcasted_iota(jnp.int32, (tm, 128), 1)
    row = i * tm + lax.broadcasted_iota(jnp.int32, (tm, 128), 0)
    kslab = kpe * cos + kpr * sin
    kslab = jnp.where(lane == MASK_LANE - 128, jnp.where(row < pad, MASK_NEG, 0.0), kslab)
    kslab = jnp.where(lane == ONES_LANE - 128, 1.0, kslab)
    k_ref[0, :, 0:128] = ckv.astype(BF16)
    k_ref[0, :, 128:256] = kslab.astype(BF16)
    q = _dot(cq.astype(BF16), wq_ref[...])
    hw = heads * 128
    for h in range(heads):
        qn = q[:, h * 128:(h + 1) * 128]
        ql = _dot(qn.astype(BF16), wuk_ref[h])
        qp = q[:, hw + h * 128:hw + (h + 1) * 128] * cos + q[:, 2 * hw + h * 128:2 * hw + (h + 1) * 128] * sin
        qp = jnp.where(lane == MASK_LANE - 128, 1.0, qp * qscale)
        q_ref[0, h, :, 0:128] = (ql * qscale).astype(BF16)
        q_ref[0, h, :, 128:256] = qp.astype(BF16)


def _flash_kernel(q_ref, k_ref, o_ref, m_sc, acc_sc, *, tq, tk, heads, n_kv):
    rows = heads * tq
    q = q_ref[0].reshape(rows, 256)
    m_sc[...] = jnp.full((rows, 128), -jnp.inf, F32)
    acc_sc[...] = jnp.zeros((rows, 256), F32)

    def body(j, carry):
        start = pl.multiple_of(j * tk, tk)
        kc = k_ref[0, pl.ds(start, tk), :]
        s = _dot_nt(q, kc)
        m_old = m_sc[...]
        m_new = jnp.maximum(m_old, jnp.max(s, axis=-1, keepdims=True))
        alpha = jnp.exp2(m_old - m_new)
        p = jnp.exp2(s - jnp.tile(m_new, (1, tk // 128)))
        acc_sc[...] = jnp.tile(alpha, (1, 2)) * acc_sc[...] + _dot(p.astype(BF16), kc)
        m_sc[...] = m_new
        return carry

    lax.fori_loop(0, n_kv, body, 0)
    acc = acc_sc[...]
    out = acc[:, 0:128] / acc[:, ONES_LANE:ONES_LANE + 1]
    for h in range(heads):
        o_ref[0, :, h * 128:(h + 1) * 128] = out[h * tq:(h + 1) * tq].astype(o_ref.dtype)


def _mla_out_kernel(ol_ref, x_ref, wuv_ref, wo_ref, g_ref, b_ref, o_ref, *, heads, alpha):
    ol = ol_ref[0]
    parts = [_dot(ol[:, h * 128:(h + 1) * 128], wuv_ref[h]).astype(BF16) for h in range(heads)]
    o = jnp.concatenate(parts, axis=-1)
    m = _dot(o, wo_ref[...])
    o_ref[0] = _layer_norm(alpha * x_ref[0] + m, g_ref[...], b_ref[...])


def _mla_layer(h, cos128, sin128, w_in, q_norm, w_q_up, kv_norm, w_kv_up, w_o, lg, lb, pad, alpha):
    B, Tp, D = h.shape
    heads = w_kv_up.shape[1]
    hw = heads * 128

    def rot(w):
        return jnp.concatenate([-w[..., MLA_ROPE // 2:], w[..., :MLA_ROPE // 2]], axis=-1)

    def slab(w):
        return jnp.concatenate([w, jnp.zeros(w.shape[:-1] + (128 - MLA_ROPE,), w.dtype)], axis=-1)

    kpe_w = w_in[:, MLA_Q_RANK + MLA_KV_RANK:]
    w_in_aug = jnp.concatenate(
        [w_in[:, :MLA_Q_RANK + MLA_KV_RANK], slab(kpe_w), slab(rot(kpe_w))], axis=1).astype(BF16)
    wq3 = w_q_up.reshape(MLA_Q_RANK, heads, MLA_NOPE + MLA_ROPE)
    wq_pe = wq3[:, :, MLA_NOPE:]
    wq_all = jnp.concatenate(
        [wq3[:, :, :MLA_NOPE].reshape(MLA_Q_RANK, hw),
         slab(wq_pe).reshape(MLA_Q_RANK, hw),
         slab(rot(wq_pe)).reshape(MLA_Q_RANK, hw)], axis=1).astype(BF16)
    w_ukT = jnp.transpose(w_kv_up[:, :, :MLA_NOPE], (1, 2, 0)).astype(BF16)
    w_uv = jnp.transpose(w_kv_up[:, :, MLA_NOPE:], (1, 0, 2)).astype(BF16)
    qscale = (MLA_NOPE + MLA_ROPE) ** -0.5 * math.log2(math.e)

    tm = _row_tile(Tp, 256)
    n = Tp // tm
    q, k = pl.pallas_call(
        functools.partial(_mla_proj_kernel, tm=tm, pad=pad, heads=heads, qscale=qscale),
        grid=(B, n),
        in_specs=[_row_spec(tm, D), _row_spec(tm, 128), _row_spec(tm, 128),
                  _const_spec(w_in_aug.shape), _const_spec((1, MLA_Q_RANK)), _const_spec((1, MLA_KV_RANK)),
                  _const_spec(wq_all.shape), _const_spec(w_ukT.shape)],
        out_specs=[pl.BlockSpec((1, heads, tm, 256), lambda b, i: (b, 0, i, 0)), _row_spec(tm, 256)],
        out_shape=[jax.ShapeDtypeStruct((B, heads, Tp, 256), BF16),
                   jax.ShapeDtypeStruct((B, Tp, 256), BF16)],
        compiler_params=_params(),
        name="mla_proj",
    )(h, cos128, sin128, w_in_aug, q_norm.reshape(1, -1), kv_norm.reshape(1, -1), wq_all, w_ukT)

    tq = 128
    tk = 1280 if Tp % 1280 == 0 else 256
    rows = heads * tq
    o_lat = pl.pallas_call(
        functools.partial(_flash_kernel, tq=tq, tk=tk, heads=heads, n_kv=Tp // tk),
        grid=(B, Tp // tq),
        in_specs=[pl.BlockSpec((1, heads, tq, 256), lambda b, i: (b, 0, i, 0)),
                  pl.BlockSpec((1, Tp, 256), lambda b, i: (b, 0, 0))],
        out_specs=_row_spec(tq, hw),
        out_shape=jax.ShapeDtypeStruct((B, Tp, hw), BF16),
        scratch_shapes=[pltpu.VMEM((rows, 128), F32), pltpu.VMEM((rows, 256), F32)],
        compiler_params=_params(),
        name="mla_flash",
    )(q, k)

    tm = _row_tile(Tp, 640)
    return pl.pallas_call(
        functools.partial(_mla_out_kernel, heads=heads, alpha=alpha),
        grid=(B, Tp // tm),
        in_specs=[_row_spec(tm, hw), _row_spec(tm, D), _const_spec(w_uv.shape),
                  _const_spec(w_o.shape), _const_spec((1, D)), _const_spec((1, D))],
        out_specs=_row_spec(tm, D),
        out_shape=jax.ShapeDtypeStruct((B, Tp, D), F32),
        compiler_params=_params(),
        name="mla_out",
    )(o_lat, h, w_uv, w_o.astype(BF16), lg.reshape(1, D), lb.reshape(1, D))


def _ffn_kernel(x_ref, xp_ref, xn_ref, wg_ref, wu_ref, cw_ref, cb_ref, wo_ref, g_ref, b_ref, o_ref,
                *, tm, pad, n_tiles, ff_chunk, alpha):
    i = pl.program_id(1)
    x = x_ref[0]
    xb = x.astype(BF16)
    ext = jnp.concatenate([xp_ref[0], x, xn_ref[0]], axis=0).astype(BF16)
    n_ext = tm + 2 * HALO
    row = i * tm - HALO + lax.broadcasted_iota(jnp.int32, (n_ext, 1), 0)
    ok = (row >= pad) & (row < n_tiles * tm)
    d_ff = wg_ref.shape[1]
    acc = jnp.zeros((tm, x.shape[1]), F32)
    for j in range(d_ff // ff_chunk):
        cs = slice(j * ff_chunk, (j + 1) * ff_chunk)
        g = jnp.where(ok, _dot(ext, wg_ref[:, cs]), 0.0)
        u = _dot(xb, wu_ref[:, cs])
        gc = (cw_ref[0:1, cs] * _shift(g, -1, tm) + cw_ref[1:2, cs] * _shift(g, 0, tm)
              + cw_ref[2:3, cs] * _shift(g, 1, tm) + cb_ref[:, cs])
        act = gc * _sigmoid(gc) * u
        acc = acc + _dot(act.astype(BF16), wo_ref[cs, :])
    o_ref[0] = _layer_norm(alpha * x + acc, g_ref[...], b_ref[...])


def _ffn_layer(h, w_in, conv_w, conv_b, w_out, lg, lb, pad, alpha):
    B, Tp, D = h.shape
    d_ff = w_out.shape[0]
    tm = _row_tile(Tp, 640)
    n = Tp // tm
    prev, nxt = _halo_specs(tm, D, Tp)
    wg = w_in[:, :d_ff].astype(BF16)
    wu = w_in[:, d_ff:].astype(BF16)
    return pl.pallas_call(
        functools.partial(_ffn_kernel, tm=tm, pad=pad, n_tiles=n, ff_chunk=256, alpha=alpha),
        grid=(B, n),
        in_specs=[_row_spec(tm, D), prev, nxt, _const_spec(wg.shape), _const_spec(wu.shape),
                  _const_spec(conv_w.shape), _const_spec((1, d_ff)), _const_spec(w_out.shape),
                  _const_spec((1, D)), _const_spec((1, D))],
        out_specs=_row_spec(tm, D),
        out_shape=jax.ShapeDtypeStruct((B, Tp, D), F32),
        compiler_params=_params(),
        name="ffn",
    )(h, h, h, wg, wu, conv_w, conv_b.reshape(1, d_ff), w_out.astype(BF16),
      lg.reshape(1, D), lb.reshape(1, D))


def _seg_sum(z, e_ref):
    hi, lo = _split2(z)
    return _dot(hi, e_ref[...]) + _dot(lo, e_ref[...])


def _seg_bcast(zs, et_ref):
    hi, lo = _split2(zs)
    return _dot(hi, et_ref[...]) + _dot(lo, et_ref[...])


def _rw_proj_kernel(x_ref, xp_ref, xn_ref, mu_ref, wr_ref, wk_ref, wv_ref, g1_ref, g2_ref,
                    w1_ref, w2_ref, w0_ref, a1_ref, a2_ref, a0_ref, kk_ref, ka_ref, rk_ref,
                    e_ref, et_ref,
                    r_out, v_out, kk_out, g_out, bonus_out, lw0_out, lw1_out, b0_out, b1_out,
                    kd0_out, kd1_out, *, tm, pad, n_tiles):
    i = pl.program_id(1)
    ext = _ext_rows(x_ref[0], xp_ref[0], xn_ref[0], i, n_tiles, tm, pad)
    x = _shift(ext, 0, tm)
    xx = 0.5 * (_shift(ext, -1, tm) + _shift(ext, 1, tm)) - x
    mix = [(x + xx * mu_ref[c:c + 1, :]).astype(BF16) for c in range(6)]
    xr, xw, xk, xv, xa, xg = mix
    row = i * tm + lax.broadcasted_iota(jnp.int32, (tm, 1), 0)
    ok = row >= pad
    r = _dot(xr, wr_ref[...])
    k = jnp.where(ok, _dot(xk, wk_ref[...]), 0.0)
    v = jnp.where(ok, _dot(xv, wv_ref[...]), 0.0)
    g = _dot(_sigmoid(_dot(xg, g1_ref[...])).astype(BF16), g2_ref[...])
    tw = jnp.tanh(_dot(xw, w1_ref[...])).astype(BF16)
    ta = _dot(xa, a1_ref[...]).astype(BF16)
    kk = k * kk_ref[...]
    ss = _seg_sum(kk * kk, e_ref)
    kk = kk * _seg_bcast(lax.rsqrt(jnp.maximum(ss, 1e-24)), et_ref)
    r_out[0] = r
    v_out[0] = v
    kk_out[0] = kk
    g_out[0] = g
    kd_sum = None
    for d, (lw_out, b_out, kd_out) in enumerate(((lw0_out, b0_out, kd0_out), (lw1_out, b1_out, kd1_out))):
        wl = w0_ref[d:d + 1, :] + _dot(tw[:, d * 128:(d + 1) * 128], w2_ref[d])
        w_log = -_softplus(-wl) - 0.5
        lw_out[0] = -jnp.exp(w_log)
        a = _sigmoid(a0_ref[d:d + 1, :] + _dot(ta[:, d * 128:(d + 1) * 128], a2_ref[d]))
        kd = k * (1.0 + (a - 1.0) * ka_ref[...])
        kd_out[0] = kd
        b_out[0] = kk * a
        kd_sum = kd if kd_sum is None else kd_sum + kd
    bs = _seg_sum(r * kd_sum * rk_ref[...], e_ref)
    bonus_out[0] = _seg_bcast(bs, et_ref) * v


def _rw_scan_kernel(r_ref, lw_ref, kk_ref, b_ref, kd_ref, v_ref, y_ref, h_sc, *, rev, n_sub):
    C = RW_CHUNK
    W = RW_GROUP * RW_HEAD
    groups = r_ref.shape[2] // W

    @pl.when(pl.program_id(1) == 0)
    def _():
        h_sc[...] = jnp.zeros_like(h_sc)

    ti = lax.broadcasted_iota(jnp.int32, (C, W), 0)
    si = lax.broadcasted_iota(jnp.int32, (C, W), 1) % RW_HEAD
    if rev:
        before = si > ti
    else:
        before = si < ti
    incl = before | (si == ti)
    eye = (si == ti).astype(F32)
    lvl_masks = []
    m = 1
    while m < C:
        same = (ti // (2 * m)) == (si // (2 * m))
        t_hi = (ti // m) % 2 == 1
        s_hi = (si // m) % 2 == 1
        if rev:
            lvl_masks.append(same & (~t_hi) & s_hi)
        else:
            lvl_masks.append(same & t_hi & (~s_hi))
        m *= 2
    bi = lax.broadcasted_iota(jnp.int32, (W, W), 0) // RW_HEAD
    bj = lax.broadcasted_iota(jnp.int32, (W, W), 1) // RW_HEAD
    bd_mask = bi == bj
    ones_bd = bd_mask.astype(BF16)
    ci =lax.broadcasted_iota(jnp.int32, (C, C), 0)
    cj = lax.broadcasted_iota(jnp.int32, (C, C), 1)
    tri = ((cj >= ci) if rev else (cj <= ci)).astype(BF16)

    def bd(z):
        zb = z.astype(BF16)
        zt = jnp.concatenate([zb] * RW_GROUP, axis=0)
        return jnp.where(bd_mask, zt, jnp.zeros_like(zt))

    def fold(z):
        zm = jnp.where(bd_mask, z, 0.0)
        out = zm[0:RW_HEAD]
        for a in range(1, RW_GROUP):
            out = out + zm[a * RW_HEAD:(a + 1) * RW_HEAD]
        return out

    def chunk(sub, carry):
        c_idx = (n_sub - 1 - sub) if rev else sub
        r0 = pl.multiple_of(c_idx * C, C)
        for gi in range(groups):
            ls = slice(gi * W, (gi + 1) * W)
            rr = r_ref[0, pl.ds(r0, C), ls]
            lw = lw_ref[0, pl.ds(r0, C), ls]
            kk = kk_ref[0, pl.ds(r0, C), ls]
            bb = b_ref[0, pl.ds(r0, C), ls]
            kd = kd_ref[0, pl.ds(r0, C), ls]
            vv = v_ref[0, pl.ds(r0, C), ls]
            H = h_sc[gi]
            l1 = lw.astype(BF16)
            rem = lw - l1.astype(F32)
            l2 = rem.astype(BF16)
            l3 = (rem - l2.astype(F32)).astype(BF16)
            gsum = _dot(tri, l1) + _dot(tri, l2) + _dot(tri, l3)
            gtot = gsum[0:1] if rev else gsum[C - 1:C]
            eg = jnp.exp(gsum)
            en = jnp.exp(-gsum)
            ec = jnp.exp(gtot - gsum)
            At = -kk * jnp.exp(gsum - lw)
            Rt = rr * eg
            Bt = bb * en
            Kt = kd * en
            lhs = jnp.concatenate([At, Rt], axis=0).astype(BF16)
            sab = _dot_nt(lhs, bd(Bt))
            sak = _dot_nt(lhs, bd(Kt))
            A_ab = sab[0:C]
            A_ak = jnp.where(before, sak[0:C], 0.0)
            A_rb = jnp.where(incl, sab[C:2 * C], 0.0)
            A_rk = jnp.where(incl, sak[C:2 * C], 0.0)
            X = eye + jnp.where(lvl_masks[0], A_ab, 0.0)
            for lm in lvl_masks[1:]:
                P = _dot(X.astype(BF16), bd(jnp.where(lm, A_ab, 0.0)))
                X = X + _dot(P.astype(BF16), bd(X))
            Hb = bd(H)
            Vb = bd(vv)
            Wm = _dot(At.astype(BF16), Hb) + _dot(A_ak.astype(BF16), Vb)
            U = _dot(X.astype(BF16), bd(Wm))
            Y = _dot(Rt.astype(BF16), Hb) + _dot(A_rb.astype(BF16), bd(U)) + _dot(A_rk.astype(BF16), Vb)
            y_ref[0, pl.ds(r0, C), ls] = Y
            gd = eye * gtot
            d1 = gd.astype(BF16)
            drem = gd - d1.astype(F32)
            d2 = drem.astype(BF16)
            d3 = (drem - d2.astype(F32)).astype(BF16)
            gam = jnp.exp(_dot(d1, ones_bd) + _dot(d2, ones_bd) + _dot(d3, ones_bd))
            lhs_t = jnp.concatenate([bb * ec, kd * ec], axis=0).astype(BF16)
            rhs_t = jnp.concatenate([U, vv], axis=0).astype(BF16)
            h_sc[gi] = gam * H + fold(_dot_tn(lhs_t, rhs_t))
        return carry

    lax.fori_loop(0, n_sub, chunk, 0)


def _rw_out_kernel(y0_ref, y1_ref, bonus_ref, g_ref, x_ref, gng_ref, gnb_ref, wo_ref, e_ref, et_ref,
                   lg_ref, lb_ref, o_ref, *, alpha):
    y = y0_ref[0] + y1_ref[0]
    inv_n = 1.0 / RW_HEAD
    mu = _seg_bcast(_seg_sum(y, e_ref) * inv_n, et_ref)
    yc = y - mu
    var = _seg_sum(yc * yc, e_ref) * inv_n
    yn = yc * _seg_bcast(lax.rsqrt(var + RW_GN_EPS), et_ref) * gng_ref[...] + gnb_ref[...]
    yo = (yn + bonus_ref[0]) * g_ref[0]
    m = _dot(yo.astype(BF16), wo_ref[...])
    o_ref[0] = _layer_norm(alpha * x_ref[0] + m, lg_ref[...], lb_ref[...])


def _rwkv_layer(h, mu, w_rkv, w0, w1, w2, a0, a1, a2, g1, g2, k_k, k_a, r_k, gn_g, gn_b, w_o,
                lg, lb, pad, alpha):
    B, Tp, D = h.shape
    heads = D // RW_HEAD

    def pad_cols(w, n):
        return jnp.concatenate([w, jnp.zeros(w.shape[:-1] + (n - w.shape[-1],), w.dtype)], axis=-1)

    def pad_rows(w, n):
        return jnp.concatenate([w, jnp.zeros(w.shape[:-2] + (n - w.shape[-2], w.shape[-1]), w.dtype)], axis=-2)

    g1p = pad_cols(g1, 256).astype(BF16)
    g2p = pad_rows(g2, 256).astype(BF16)
    w1p = jnp.concatenate([pad_cols(w1[0], 128), pad_cols(w1[1], 128)], axis=1).astype(BF16)
    a1p = jnp.concatenate([pad_cols(a1[0], 128), pad_cols(a1[1], 128)], axis=1).astype(BF16)
    w2p = pad_rows(w2, 128).astype(BF16)
    a2p = pad_rows(a2, 128).astype(BF16)
    e = (jnp.arange(D)[:, None] // RW_HEAD == jnp.arange(128)[None, :]).astype(BF16)
    et = e.T

    tm = _row_tile(Tp, 256)
    n = Tp // tm
    prev, nxt = _halo_specs(tm, D, Tp)
    vec = _const_spec((1, D))
    outs = pl.pallas_call(
        functools.partial(_rw_proj_kernel, tm=tm, pad=pad, n_tiles=n),
        grid=(B, n),
        in_specs=[_row_spec(tm, D), prev, nxt, _const_spec((6, D)),
                  _const_spec((D, D)), _const_spec((D, D)), _const_spec((D, D)),
                  _const_spec(g1p.shape), _const_spec(g2p.shape),
                  _const_spec(w1p.shape), _const_spec(w2p.shape), _const_spec((2, D)),
                  _const_spec(a1p.shape), _const_spec(a2p.shape), _const_spec((2, D)),
                  vec, vec, vec, _const_spec(e.shape), _const_spec(et.shape)],
        out_specs=[_row_spec(tm, D)] * 11,
        out_shape=[jax.ShapeDtypeStruct((B, Tp, D), F32)] * 11,
        compiler_params=_params(),
        name="rw_proj",
    )(h, h, h, mu, w_rkv[0].astype(BF16), w_rkv[1].astype(BF16), w_rkv[2].astype(BF16), g1p, g2p,
      w1p, w2p, w0, a1p, a2p, a0, k_k.reshape(1, D), k_a.reshape(1, D), r_k.reshape(1, D), e, et)
    r, v, kk, g, bonus, lw0, lw1, b0, b1, kd0, kd1 = outs

    ts = _row_tile(Tp, 256)
    ns = Tp // ts
    ys = []
    for rev, lw, bb, kd in ((False, lw0, b0, kd0), (True, lw1, b1, kd1)):
        if rev:
            spec = pl.BlockSpec((1, ts, D), lambda b, i: (b, ns - 1 - i, 0))
        else:
            spec = _row_spec(ts, D)
        ys.append(pl.pallas_call(
            functools.partial(_rw_scan_kernel, rev=rev, n_sub=ts // RW_CHUNK),
            grid=(B, ns),
            in_specs=[spec] * 6,
            out_specs=spec,
            out_shape=jax.ShapeDtypeStruct((B, Tp, D), F32),
            scratch_shapes=[pltpu.VMEM((D // (RW_GROUP * RW_HEAD), RW_HEAD, RW_GROUP * RW_HEAD), F32)],
            compiler_params=pltpu.CompilerParams(
                dimension_semantics=("parallel", "arbitrary"), vmem_limit_bytes=VMEM_LIMIT),
            name="rw_scan_bwd" if rev else "rw_scan_fwd",
        )(r, lw, kk, bb, kd, v))

    tm = _row_tile(Tp, 640)
    return pl.pallas_call(
        functools.partial(_rw_out_kernel, alpha=alpha),
        grid=(B, Tp // tm),
        in_specs=[_row_spec(tm, D)] * 5 + [vec, vec, _const_spec((D, D)), _const_spec(e.shape),
                                            _const_spec(et.shape), vec, vec],
        out_specs=_row_spec(tm, D),
        out_shape=jax.ShapeDtypeStruct((B, Tp, D), F32),
        compiler_params=_params(),
        name="rw_out",
    )(ys[0], ys[1], bonus, g, h, gn_g.reshape(1, D), gn_b.reshape(1, D), w_o.astype(BF16), e, et,
      lg.reshape(1, D), lb.reshape(1, D))


def _gelu_tanh(z):
    return 0.5 * z * (1.0 + jnp.tanh(math.sqrt(2.0 / math.pi) * (z + 0.044715 * z * z * z)))


def _lru_proj_kernel(x_ref, wg_ref, wu_ref, gate_out, u_out, *, tm, pad):
    i = pl.program_id(1)
    xb = x_ref[0].astype(BF16)
    gate_out[0] = _gelu_tanh(_dot(xb, wg_ref[...]))
    row = i * tm + lax.broadcasted_iota(jnp.int32, (tm, 1), 0)
    u_out[0] = jnp.where(row >= pad, _dot(xb, wu_ref[...]), 0.0)


def _lru_scan_kernel(u_ref, up_ref, un_ref, cw_ref, cb_ref, gw_ref, gb_ref, lam_ref, h_out,
                     a_sc, b_sc, h_sc, *, tm, pad, n_tiles, rev):
    step = pl.program_id(1)
    i = (n_tiles - 1 - step) if rev else step

    @pl.when(step == 0)
    def _():
        h_sc[...] = jnp.zeros_like(h_sc)

    ext = _ext_rows(u_ref[0], up_ref[0], un_ref[0], i, n_tiles, tm, 0)
    uc = cb_ref[...]
    for kt in range(cw_ref.shape[0]):
        uc = uc + cw_ref[kt:kt + 1, :] * _shift(ext, kt - 2, tm)
    width = uc.shape[1]
    nblk = width // LRU_BLOCK
    sp = _softplus(-lam_ref[...])
    row = i * tm + lax.broadcasted_iota(jnp.int32, (tm, 1), 0)
    ok = row >= pad
    for nb in range(nblk):
        cs = slice(nb * LRU_BLOCK, (nb + 1) * LRU_BLOCK)
        ub = uc[:, cs]
        ubb = ub.astype(BF16)
        rg = _sigmoid(_dot(ubb, gw_ref[0, nb]) + gb_ref[0:1, cs])
        ig = _sigmoid(_dot(ubb, gw_ref[1, nb]) + gb_ref[1:2, cs])
        a = jnp.exp(-LRU_C * rg * sp[:, cs])
        a_sc[:, cs] = a
        b_sc[:, cs] = jnp.where(ok, jnp.sqrt(1.0 - a * a) * (ig * ub), 0.0)

    def body(s, hcur):
        t = (tm - 1 - s) if rev else s
        hnew = a_sc[pl.ds(t, 1), :] * hcur + b_sc[pl.ds(t, 1), :]
        h_out[0, pl.ds(t, 1), :] = hnew
        return hnew

    h_sc[...] = lax.fori_loop(0, tm, body, h_sc[...], unroll=8)


def _lru_out_kernel(h0_ref, h1_ref, gate_ref, x_ref, wo_ref, lg_ref, lb_ref, o_ref, *, alpha):
    hh = (h0_ref[0] + h1_ref[0]) * gate_ref[0]
    m = _dot(hh.astype(BF16), wo_ref[...])
    o_ref[0] = _layer_norm(alpha * x_ref[0] + m, lg_ref[...], lb_ref[...])


def _lru_layer(h, w_in, conv_w, conv_b, gate_w, gate_b, lam, w_o, lg, lb, pad, alpha):
    B, Tp, D = h.shape
    width = w_o.shape[0]
    tm = _row_tile(Tp, 640)
    n = Tp // tm
    gate, u = pl.pallas_call(
        functools.partial(_lru_proj_kernel, tm=tm, pad=pad),
        grid=(B, n),
        in_specs=[_row_spec(tm, D), _const_spec((D, width)), _const_spec((D, width))],
        out_specs=[_row_spec(tm, width)] * 2,
        out_shape=[jax.ShapeDtypeStruct((B, Tp, width), F32)] * 2,
        compiler_params=_params(),
        name="lru_proj",
    )(h, w_in[:, :width].astype(BF16), w_in[:, width:].astype(BF16))

    ts = _row_tile(Tp, 256)
    ns = Tp // ts
    nb8 = ts // HALO
    last = Tp // HALO - 1
    hs = []
    for d, rev in ((0, False), (1, True)):
        if rev:
            cur = pl.BlockSpec((1, ts, width), lambda b, s: (b, ns - 1 - s, 0))
            prev = pl.BlockSpec((1, HALO, width), lambda b, s: (b, jnp.maximum((ns - 1 - s) * nb8 - 1, 0), 0))
            nxt = pl.BlockSpec((1, HALO, width), lambda b, s: (b, jnp.minimum((ns - s) * nb8, last), 0))
        else:
            cur = _row_spec(ts, width)
            prev, nxt = _halo_specs(ts, width, Tp)
        hs.append(pl.pallas_call(
            functools.partial(_lru_scan_kernel, tm=ts, pad=pad, n_tiles=ns, rev=rev),
            grid=(B, ns),
            in_specs=[cur, prev, nxt, _const_spec(conv_w.shape), _const_spec((1, width)),
                      _const_spec(gate_w.shape[1:]), _const_spec((2, width)), _const_spec((1, width))],
            out_specs=cur,
            out_shape=jax.ShapeDtypeStruct((B, Tp, width), F32),
            scratch_shapes=[pltpu.VMEM((ts, width), F32), pltpu.VMEM((ts, width), F32),
                            pltpu.VMEM((1, width), F32)],
            compiler_params=pltpu.CompilerParams(
                dimension_semantics=("parallel", "arbitrary"), vmem_limit_bytes=VMEM_LIMIT),
            name="lru_scan_bwd" if rev else "lru_scan_fwd",
        )(u, u, u, conv_w, conv_b.reshape(1, width), gate_w[d].astype(BF16), gate_b[d],
          lam[d].reshape(1, width)))

    return pl.pallas_call(
        functools.partial(_lru_out_kernel, alpha=alpha),
        grid=(B, n),
        in_specs=[_row_spec(tm, width)] * 3 + [_row_spec(tm, D), _const_spec((width, D)),
                                               _const_spec((1, D)), _const_spec((1, D))],
        out_specs=_row_spec(tm, D),
        out_shape=jax.ShapeDtypeStruct((B, Tp, D), F32),
        compiler_params=_params(),
        name="lru_out",
    )(hs[0], hs[1], gate, h, w_o.astype(BF16), lg.reshape(1, D), lb.reshape(1, D))


def kernel(x, positions, meta_tokens, ln_g, ln_b, ffn_w_in, ffn_conv_w, ffn_conv_b, ffn_w_out, mla_w_in, mla_q_norm, mla_w_q_up, mla_kv_norm, mla_w_kv_up, mla_w_o, rw_mu, rw_w_rkv, rw_w0, rw_w1, rw_w2, rw_a0, rw_a1, rw_a2, rw_g1, rw_g2, rw_k_k, rw_k_a, rw_r_k, rw_gn_g, rw_gn_b, rw_w_o, lru_w_in, lru_conv_w, lru_conv_b, lru_gate_w, lru_gate_b, lru_lambda, lru_w_o):
    B, S, D = x.shape
    depth = ln_g.shape[0]
    T = S + N_META
    Tp = -(-T // SEQ_ALIGN) * SEQ_ALIGN
    pad = Tp - T
    alpha = (2.0 * depth) ** 0.25
    dt = x.dtype
    h = jnp.concatenate([jnp.zeros((B, pad, D), dt),
                         jnp.broadcast_to(meta_tokens[None].astype(dt), (B, N_META, D)), x], axis=1)
    pos = jnp.concatenate(
        [jnp.zeros((B, pad), jnp.int32),
         jnp.broadcast_to(jnp.arange(N_META, dtype=jnp.int32)[None, :], (B, N_META)),
         positions + N_META], axis=1)
    inv_freq = ROPE_BASE ** (-jnp.arange(0, MLA_ROPE, 2, dtype=F32) / MLA_ROPE)
    ang = pos.astype(F32)[..., None] * inv_freq
    zeros = jnp.zeros((B, Tp, 128 - MLA_ROPE), F32)
    cos = jnp.cos(ang)
    sin = jnp.sin(ang)
    cos128 = jnp.concatenate([cos, cos, zeros], axis=-1)
    sin128 = jnp.concatenate([sin, sin, zeros], axis=-1)
    for i in range(depth):
        kind = i % 3
        j = i // 3
        lg, lb = ln_g[i, 0], ln_b[i, 0]
        if kind == 0:
            h = _mla_layer(h, cos128, sin128, mla_w_in[j], mla_q_norm[j], mla_w_q_up[j], mla_kv_norm[j],
                           mla_w_kv_up[j], mla_w_o[j], lg, lb, pad, alpha)
        elif kind == 1:
            h = _rwkv_layer(h, rw_mu[j], rw_w_rkv[j], rw_w0[j], rw_w1[j], rw_w2[j], rw_a0[j], rw_a1[j],
                            rw_a2[j], rw_g1[j], rw_g2[j], rw_k_k[j], rw_k_a[j], rw_r_k[j], rw_gn_g[j],
                            rw_gn_b[j], rw_w_o[j], lg, lb, pad, alpha)
        else:
            h = _lru_layer(h, lru_w_in[j], lru_conv_w[j], lru_conv_b[j], lru_gate_w[j], lru_gate_b[j],
                           lru_lambda[j], lru_w_o[j], lg, lb, pad, alpha)
        h = _ffn_layer(h, ffn_w_in[i], ffn_conv_w[i], ffn_conv_b[i], ffn_w_out[i],
                       ln_g[i, 1], ln_b[i, 1], pad, alpha)
    return h[:, pad + N_META:]
```

```python
import functools
import math

import jax
import jax.numpy as jnp
from jax import lax
from jax.experimental import pallas as pl
from jax.experimental.pallas import tpu as pltpu

F32 = jnp.float32
BF16 = jnp.bfloat16

N_META = 16
LN_EPS = 1e-5
RMS_EPS = 1e-6
ROPE_BASE = 10000.0
MLA_NOPE = 128
MLA_ROPE = 64
MLA_V = 128
MLA_Q_RANK = 256
MLA_KV_RANK = 128
RW_HEAD = 64
RW_GN_EPS = 64e-5
LRU_C = 8.0
LRU_BLOCK = 256
SEQ_ALIGN = 256
HALO = 8
RW_CHUNK = 64
RW_GROUP = 4
MASK_NEG = -1e30
MASK_LANE = 128 + MLA_ROPE
ONES_LANE = MASK_LANE + 1
VMEM_LIMIT = 56 * 1024 * 1024


def _row_tile(tp, target):
    best = 128
    for t in range(128, min(tp, target) + 1, 128):
        if tp % t == 0:
            best = t
    return best


def _const_spec(shape):
    nd = len(shape)
    return pl.BlockSpec(shape, lambda *_: (0,) * nd, pipeline_mode=pl.Buffered(1))


def _row_spec(tm, c):
    return pl.BlockSpec((1, tm, c), lambda b, i: (b, i, 0))


def _halo_specs(tm, c, tp):
    nb = tm // HALO
    last = tp // HALO - 1
    prev = pl.BlockSpec((1, HALO, c), lambda b, i: (b, jnp.maximum(i * nb - 1, 0), 0))
    nxt = pl.BlockSpec((1, HALO, c), lambda b, i: (b, jnp.minimum((i + 1) * nb, last), 0))
    return prev, nxt


def _params(n_parallel=2):
    return pltpu.CompilerParams(
        dimension_semantics=("parallel",) * n_parallel,
        vmem_limit_bytes=VMEM_LIMIT)


def _dot(a, b):
    return jnp.dot(a, b, preferred_element_type=F32)


def _dot_nt(a, b):
    return lax.dot_general(a, b, (((1,), (1,)), ((), ())), preferred_element_type=F32)


def _dot_tn(a, b):
    return lax.dot_general(a, b, (((0,), (0,)), ((), ())), preferred_element_type=F32)


def _layer_norm(z, g, b):
    mu = jnp.mean(z, axis=-1, keepdims=True)
    zc = z - mu
    var = jnp.mean(zc * zc, axis=-1, keepdims=True)
    return zc * lax.rsqrt(var + LN_EPS) * g + b


def _sigmoid(z):
    return 1.0 / (1.0 + jnp.exp(-z))


def _softplus(z):
    return jnp.maximum(z, 0.0) + jnp.log(1.0 + jnp.exp(-jnp.abs(z)))


def _split2(z):
    hi = z.astype(BF16)
    lo = (z - hi.astype(F32)).astype(BF16)
    return hi, lo


def _ext_rows(x, prev, nxt, i, n_tiles, tm, pad):
    ext = jnp.concatenate([prev, x, nxt], axis=0)
    row = i * tm - HALO + lax.broadcasted_iota(jnp.int32, (tm + 2 * HALO, 1), 0)
    ok = (row >= pad) & (row < n_tiles * tm)
    return jnp.where(ok, ext, 0.0)


def _shift(ext, k, tm):
    n = ext.shape[0]
    if k == 0:
        return ext[HALO:HALO + tm]
    return pltpu.roll(ext, (-k) % n, 0)[HALO:HALO + tm]


def _mla_proj_kernel(x_ref, cos_ref, sin_ref, w_in_ref, qn_ref, kvn_ref, wq_ref, wuk_ref,
                     q_ref, k_ref, *, tm, pad, heads, qscale):
    i = pl.program_id(1)
    x = x_ref[0]
    hp = _dot(x.astype(BF16), w_in_ref[...])
    cq = hp[:, :MLA_Q_RANK]
    ckv = hp[:, MLA_Q_RANK:MLA_Q_RANK + MLA_KV_RANK]
    kpe = hp[:, 384:512]
    kpr = hp[:, 512:640]
    cq = cq * lax.rsqrt(jnp.mean(cq * cq, axis=-1, keepdims=True) + RMS_EPS) * qn_ref[...]
    ckv = ckv * lax.rsqrt(jnp.mean(ckv * ckv, axis=-1, keepdims=True) + RMS_EPS) * kvn_ref[...]
    cos = cos_ref[0]
    sin = sin_ref[0]
    lane = lax.broadcasted_iota(jnp.int32, (tm, 128), 1)
    row = i * tm + lax.broadcasted_iota(jnp.int32, (tm, 128), 0)
    kslab = kpe * cos + kpr * sin
    kslab = jnp.where(lane == MASK_LANE - 128, jnp.where(row < pad, MASK_NEG, 0.0), kslab)
    kslab = jnp.where(lane == ONES_LANE - 128, 1.0, kslab)
    k_ref[0, :, 0:128] = ckv.astype(BF16)
    k_ref[0, :, 128:256] = kslab.astype(BF16)
    q = _dot(cq.astype(BF16), wq_ref[...])
    hw = heads * 128
    for h in range(heads):
        qn = q[:, h * 128:(h + 1) * 128]
        ql = _dot(qn.astype(BF16), wuk_ref[h])
        qp = q[:, hw + h * 128:hw + (h + 1) * 128] * cos + q[:, 2 * hw + h * 128:2 * hw + (h + 1) * 128] * sin
        qp = jnp.where(lane == MASK_LANE - 128, 1.0, qp * qscale)
        q_ref[0, h, :, 0:128] = (ql * qscale).astype(BF16)
        q_ref[0, h, :, 128:256] = qp.astype(BF16)


def _flash_kernel(q_ref, k_ref, o_ref, m_sc, acc_sc, sa_sc, sb_sc, mca_sc, mcb_sc, *, tq, tk, heads, n_kv):
    rows = heads * tq
    n_t = tk // 256
    q = q_ref[0].reshape(rows, 256)
    m_sc[...] = jnp.full((rows, 128), -jnp.inf, F32)
    acc_sc[...] = jnp.zeros((rows, 256), F32)

    def key_chunk(j):
        return k_ref[0, pl.ds(pl.multiple_of(j * tk, tk), tk), :]

    def scores(j, slot):
        s_ref, mc_ref = slot
        kc = key_chunk(j)
        mc = None
        for t in range(n_t):
            st = _dot_nt(q, kc[t * 256:(t + 1) * 256])
            s_ref[t] = st
            mt = jnp.maximum(st[:, 0:128], st[:, 128:256])
            mc = mt if mc is None else jnp.maximum(mc, mt)
        mc_ref[...] = mc

    def softmax_values(j, slot):
        s_ref, mc_ref = slot
        kc = key_chunk(j)
        m_old = m_sc[...]
        m_cur = jnp.max(mc_ref[...], axis=-1, keepdims=True)
        m_new = jnp.maximum(m_old, jnp.broadcast_to(m_cur, (rows, 128)))
        alpha = jnp.exp2(m_old - m_new)
        m2 = jnp.tile(m_new, (1, 2))
        p = jnp.concatenate([jnp.exp2((s_ref[t] - m2).astype(BF16)) for t in range(n_t)], axis=1)
        acc_sc[...] = jnp.tile(alpha, (1, 2)) * acc_sc[...] + _dot(p, kc)
        m_sc[...] = m_new

    slot_a = (sa_sc, mca_sc)
    slot_b = (sb_sc, mcb_sc)
    scores(0, slot_a)

    def body(i, carry):
        j = 2 * i
        scores(j + 1, slot_b)
        softmax_values(j, slot_a)
        scores(j + 2, slot_a)
        softmax_values(j + 1, slot_b)
        return carry

    n_pairs = (n_kv - 1) // 2
    lax.fori_loop(0, n_pairs, body, 0)
    if n_kv % 2 == 1:
        softmax_values(n_kv - 1, slot_a)
    else:
        scores(n_kv - 1, slot_b)
        softmax_values(n_kv - 2, slot_a)
        softmax_values(n_kv - 1, slot_b)
    acc = acc_sc[...]
    out = acc[:, 0:128] / acc[:, ONES_LANE:ONES_LANE + 1]
    for h in range(heads):
        o_ref[0, :, h * 128:(h + 1) * 128] = out[h * tq:(h + 1) * tq].astype(o_ref.dtype)


def _mla_out_kernel(ol_ref, x_ref, wuv_ref, wo_ref, g_ref, b_ref, o_ref, *, heads, alpha):
    ol = ol_ref[0]
    parts = [_dot(ol[:, h * 128:(h + 1) * 128], wuv_ref[h]).astype(BF16) for h in range(heads)]
    o = jnp.concatenate(parts, axis=-1)
    m = _dot(o, wo_ref[...])
    o_ref[0] = _layer_norm(alpha * x_ref[0] + m, g_ref[...], b_ref[...])


def _mla_layer(h, cos128, sin128, w_in, q_norm, w_q_up, kv_norm, w_kv_up, w_o, lg, lb, pad, alpha):
    B, Tp, D = h.shape
    heads = w_kv_up.shape[1]
    hw = heads * 128

    def rot(w):
        return jnp.concatenate([-w[..., MLA_ROPE // 2:], w[..., :MLA_ROPE // 2]], axis=-1)

    def slab(w):
        return jnp.concatenate([w, jnp.zeros(w.shape[:-1] + (128 - MLA_ROPE,), w.dtype)], axis=-1)

    kpe_w = w_in[:, MLA_Q_RANK + MLA_KV_RANK:]
    w_in_aug = jnp.concatenate(
        [w_in[:, :MLA_Q_RANK + MLA_KV_RANK], slab(kpe_w), slab(rot(kpe_w))], axis=1).astype(BF16)
    wq3 = w_q_up.reshape(MLA_Q_RANK, heads, MLA_NOPE + MLA_ROPE)
    wq_pe = wq3[:, :, MLA_NOPE:]
    wq_all = jnp.concatenate(
        [wq3[:, :, :MLA_NOPE].reshape(MLA_Q_RANK, hw),
         slab(wq_pe).reshape(MLA_Q_RANK, hw),
         slab(rot(wq_pe)).reshape(MLA_Q_RANK, hw)], axis=1).astype(BF16)
    w_ukT = jnp.transpose(w_kv_up[:, :, :MLA_NOPE], (1, 2, 0)).astype(BF16)
    w_uv = jnp.transpose(w_kv_up[:, :, MLA_NOPE:], (1, 0, 2)).astype(BF16)
    qscale = (MLA_NOPE + MLA_ROPE) ** -0.5 * math.log2(math.e)

    tm = _row_tile(Tp, 256)
    n = Tp // tm
    q, k = pl.pallas_call(
        functools.partial(_mla_proj_kernel, tm=tm, pad=pad, heads=heads, qscale=qscale),
        grid=(B, n),
        in_specs=[_row_spec(tm, D), _row_spec(tm, 128), _row_spec(tm, 128),
                  _const_spec(w_in_aug.shape), _const_spec((1, MLA_Q_RANK)), _const_spec((1, MLA_KV_RANK)),
                  _const_spec(wq_all.shape), _const_spec(w_ukT.shape)],
        out_specs=[pl.BlockSpec((1, heads, tm, 256), lambda b, i: (b, 0, i, 0)), _row_spec(tm, 256)],
        out_shape=[jax.ShapeDtypeStruct((B, heads, Tp, 256), BF16),
                   jax.ShapeDtypeStruct((B, Tp, 256), BF16)],
        compiler_params=_params(),
        name="mla_proj",
    )(h, cos128, sin128, w_in_aug, q_norm.reshape(1, -1), kv_norm.reshape(1, -1), wq_all, w_ukT)

    tq = 64
    tk = 1280 if Tp % 1280 == 0 else 256
    rows = heads * tq
    o_lat = pl.pallas_call(
        functools.partial(_flash_kernel, tq=tq, tk=tk, heads=heads, n_kv=Tp // tk),
        grid=(B, Tp // tq),
        in_specs=[pl.BlockSpec((1, heads, tq, 256), lambda b, i: (b, 0, i, 0)),
                  pl.BlockSpec((1, Tp, 256), lambda b, i: (b, 0, 0))],
        out_specs=_row_spec(tq, hw),
        out_shape=jax.ShapeDtypeStruct((B, Tp, hw), BF16),
        scratch_shapes=[pltpu.VMEM((rows, 128), F32), pltpu.VMEM((rows, 256), F32),
                        pltpu.VMEM((tk // 256, rows, 256), F32), pltpu.VMEM((tk // 256, rows, 256), F32),
                        pltpu.VMEM((rows, 128), F32), pltpu.VMEM((rows, 128), F32)],
        compiler_params=_params(),
        name="mla_flash",
    )(q, k)

    tm = _row_tile(Tp, 640)
    return pl.pallas_call(
        functools.partial(_mla_out_kernel, heads=heads, alpha=alpha),
        grid=(B, Tp // tm),
        in_specs=[_row_spec(tm, hw), _row_spec(tm, D), _const_spec(w_uv.shape),
                  _const_spec(w_o.shape), _const_spec((1, D)), _const_spec((1, D))],
        out_specs=_row_spec(tm, D),
        out_shape=jax.ShapeDtypeStruct((B, Tp, D), F32),
        compiler_params=_params(),
        name="mla_out",
    )(o_lat, h, w_uv, w_o.astype(BF16), lg.reshape(1, D), lb.reshape(1, D))


def _ffn_kernel(x_ref, xp_ref, xn_ref, wg_ref, wu_ref, cw_ref, cb_ref, wo_ref, g_ref, b_ref, o_ref,
                *, tm, pad, n_tiles, ff_chunk, alpha):
    i = pl.program_id(1)
    x = x_ref[0]
    xb = x.astype(BF16)
    ext = jnp.concatenate([xp_ref[0], x, xn_ref[0]], axis=0).astype(BF16)
    n_ext = tm + 2 * HALO
    row = i * tm - HALO + lax.broadcasted_iota(jnp.int32, (n_ext, 1), 0)
    ok = (row >= pad) & (row < n_tiles * tm)
    d_ff = wg_ref.shape[1]
    acc = jnp.zeros((tm, x.shape[1]), F32)
    for j in range(d_ff // ff_chunk):
        cs = slice(j * ff_chunk, (j + 1) * ff_chunk)
        g = jnp.where(ok, _dot(ext, wg_ref[:, cs]), 0.0)
        u = _dot(xb, wu_ref[:, cs])
        gc = (cw_ref[0:1, cs] * _shift(g, -1, tm) + cw_ref[1:2, cs] * _shift(g, 0, tm)
              + cw_ref[2:3, cs] * _shift(g, 1, tm) + cb_ref[:, cs])
        act = gc * _sigmoid(gc) * u
        acc = acc + _dot(act.astype(BF16), wo_ref[cs, :])
    o_ref[0] = _layer_norm(alpha * x + acc, g_ref[...], b_ref[...])


def _ffn_layer(h, w_in, conv_w, conv_b, w_out, lg, lb, pad, alpha):
    B, Tp, D = h.shape
    d_ff = w_out.shape[0]
    tm = _row_tile(Tp, 640)
    n = Tp // tm
    prev, nxt = _halo_specs(tm, D, Tp)
    wg = w_in[:, :d_ff].astype(BF16)
    wu = w_in[:, d_ff:].astype(BF16)
    return pl.pallas_call(
        functools.partial(_ffn_kernel, tm=tm, pad=pad, n_tiles=n, ff_chunk=256, alpha=alpha),
        grid=(B, n),
        in_specs=[_row_spec(tm, D), prev, nxt, _const_spec(wg.shape), _const_spec(wu.shape),
                  _const_spec(conv_w.shape), _const_spec((1, d_ff)), _const_spec(w_out.shape),
                  _const_spec((1, D)), _const_spec((1, D))],
        out_specs=_row_spec(tm, D),
        out_shape=jax.ShapeDtypeStruct((B, Tp, D), F32),
        compiler_params=_params(),
        name="ffn",
    )(h, h, h, wg, wu, conv_w, conv_b.reshape(1, d_ff), w_out.astype(BF16),
      lg.reshape(1, D), lb.reshape(1, D))


def _seg_sum(z, e_ref):
    hi, lo = _split2(z)
    return _dot(hi, e_ref[...]) + _dot(lo, e_ref[...])


def _seg_bcast(zs, et_ref):
    hi, lo = _split2(zs)
    return _dot(hi, et_ref[...]) + _dot(lo, et_ref[...])


def _rw_proj_kernel(x_ref, xp_ref, xn_ref, mu_ref, wr_ref, wk_ref, wv_ref, g1_ref, g2_ref,
                    w1_ref, w2_ref, w0_ref, a1_ref, a2_ref, a0_ref, kk_ref, ka_ref, rk_ref,
                    e_ref, et_ref,
                    r_out, v_out, kk_out, g_out, bonus_out, lw0_out, lw1_out, b0_out, b1_out,
                    kd0_out, kd1_out, *, tm, pad, n_tiles):
    i = pl.program_id(1)
    ext = _ext_rows(x_ref[0], xp_ref[0], xn_ref[0], i, n_tiles, tm, pad)
    x = _shift(ext, 0, tm)
    xx = 0.5 * (_shift(ext, -1, tm) + _shift(ext, 1, tm)) - x
    mix = [(x + xx * mu_ref[c:c + 1, :]).astype(BF16) for c in range(6)]
    xr, xw, xk, xv, xa, xg = mix
    row = i * tm + lax.broadcasted_iota(jnp.int32, (tm, 1), 0)
    ok = row >= pad
    r = _dot(xr, wr_ref[...])
    k = jnp.where(ok, _dot(xk, wk_ref[...]), 0.0)
    v = jnp.where(ok, _dot(xv, wv_ref[...]), 0.0)
    g = _dot(_sigmoid(_dot(xg, g1_ref[...])).astype(BF16), g2_ref[...])
    tw = jnp.tanh(_dot(xw, w1_ref[...])).astype(BF16)
    ta = _dot(xa, a1_ref[...]).astype(BF16)
    kk = k * kk_ref[...]
    ss = _seg_sum(kk * kk, e_ref)
    kk = kk * _seg_bcast(lax.rsqrt(jnp.maximum(ss, 1e-24)), et_ref)
    r_out[0] = r
    v_out[0] = v
    kk_out[0] = kk
    g_out[0] = g
    kd_sum = None
    for d, (lw_out, b_out, kd_out) in enumerate(((lw0_out, b0_out, kd0_out), (lw1_out, b1_out, kd1_out))):
        wl = w0_ref[d:d + 1, :] + _dot(tw[:, d * 128:(d + 1) * 128], w2_ref[d])
        w_log = -_softplus(-wl) - 0.5
        lw_out[0] = -jnp.exp(w_log)
        a = _sigmoid(a0_ref[d:d + 1, :] + _dot(ta[:, d * 128:(d + 1) * 128], a2_ref[d]))
        kd = k * (1.0 + (a - 1.0) * ka_ref[...])
        kd_out[0] = kd
        b_out[0] = kk * a
        kd_sum = kd if kd_sum is None else kd_sum + kd
    bs = _seg_sum(r * kd_sum * rk_ref[...], e_ref)
    bonus_out[0] = _seg_bcast(bs, et_ref) * v


def _rw_scan_kernel(r_ref, lw_ref, kk_ref, b_ref, kd_ref, v_ref, y_ref, h_sc, *, rev, n_sub):
    C = RW_CHUNK
    W = RW_GROUP * RW_HEAD
    groups = r_ref.shape[2] // W

    @pl.when(pl.program_id(1) == 0)
    def _():
        h_sc[...] = jnp.zeros_like(h_sc)

    ti = lax.broadcasted_iota(jnp.int32, (C, W), 0)
    si = lax.broadcasted_iota(jnp.int32, (C, W), 1) % RW_HEAD
    if rev:
        before = si > ti
    else:
        before = si < ti
    incl = before | (si == ti)
    eye = (si == ti).astype(F32)
    lvl_masks = []
    m = 1
    while m < C:
        same = (ti // (2 * m)) == (si // (2 * m))
        t_hi = (ti // m) % 2 == 1
        s_hi = (si // m) % 2 == 1
        if rev:
            lvl_masks.append(same & (~t_hi) & s_hi)
        else:
            lvl_masks.append(same & t_hi & (~s_hi))
        m *= 2
    bi = lax.broadcasted_iota(jnp.int32, (W, W), 0) // RW_HEAD
    bj = lax.broadcasted_iota(jnp.int32, (W, W), 1) // RW_HEAD
    bd_mask = bi == bj
    ones_bd = bd_mask.astype(BF16)
    ci =lax.broadcasted_iota(jnp.int32, (C, C), 0)
    cj = lax.broadcasted_iota(jnp.int32, (C, C), 1)
    tri = ((cj >= ci) if rev else (cj <= ci)).astype(BF16)

    def bd(z):
        zb = z.astype(BF16)
        zt = jnp.concatenate([zb] * RW_GROUP, axis=0)
        return jnp.where(bd_mask, zt, jnp.zeros_like(zt))

    def fold(z):
        zm = jnp.where(bd_mask, z, 0.0)
        out = zm[0:RW_HEAD]
        for a in range(1, RW_GROUP):
            out = out + zm[a * RW_HEAD:(a + 1) * RW_HEAD]
        return out

    order = list(range(n_sub - 1, -1, -1)) if rev else list(range(n_sub))
    units = [(c, gi) for c in order for gi in range(groups)]

    def load(ref, u):
        c, gi = u
        return ref[0, c * C:(c + 1) * C, gi * W:(gi + 1) * W]

    def split3(z):
        z1 = z.astype(BF16)
        rem = z - z1.astype(F32)
        z2 = rem.astype(BF16)
        z3 = (rem - z2.astype(F32)).astype(BF16)
        return z1, z2, z3

    gsum, gtot = {}, {}
    for u in units:
        l1, l2, l3 = split3(load(lw_ref, u))
        gs = _dot(tri, l1) + _dot(tri, l2) + _dot(tri, l3)
        gsum[u] = gs
        gtot[u] = gs[0:1] if rev else gs[C - 1:C]
    lhs, Bt, Kt, lhs_t, gam = {}, {}, {}, {}, {}
    for u in units:
        gs = gsum[u]
        bb = load(b_ref, u)
        kd = load(kd_ref, u)
        en = jnp.exp(-gs)
        ec = jnp.exp(gtot[u] - gs)
        At = -load(kk_ref, u) * jnp.exp(gs - load(lw_ref, u))
        Rt = load(r_ref, u) * jnp.exp(gs)
        lhs[u] = jnp.concatenate([At, Rt], axis=0).astype(BF16)
        Bt[u] = bb * en
        Kt[u] = kd * en
        lhs_t[u] = jnp.concatenate([bb * ec, kd * ec], axis=0).astype(BF16)
        d1, d2, d3 = split3(eye * gtot[u])
        dd = _dot(jnp.concatenate([d1, d2, d3], axis=0), ones_bd)
        gam[u] = jnp.exp(dd[0:C] + dd[C:2 * C] + dd[2 * C:3 * C])
    A_ab, AA, A_rb = {}, {}, {}
    for u in units:
        sab = _dot_nt(lhs[u], bd(Bt[u]))
        sak = _dot_nt(lhs[u], bd(Kt[u]))
        A_ab[u] = sab[0:C]
        A_rb[u] = jnp.where(incl, sab[C:2 * C], 0.0).astype(BF16)
        AA[u] = jnp.concatenate([jnp.where(before, sak[0:C], 0.0),
                                 jnp.where(incl, sak[C:2 * C], 0.0)], axis=0).astype(BF16)
    X = {u: eye + jnp.where(lvl_masks[0], A_ab[u], 0.0) for u in units}
    for lm in lvl_masks[1:]:
        P = {u: _dot(X[u].astype(BF16), bd(jnp.where(lm, A_ab[u], 0.0))) for u in units}
        X = {u: X[u] + _dot(P[u].astype(BF16), bd(X[u])) for u in units}

    H = [h_sc[gi] for gi in range(groups)]
    for c in order:
        us = [(c, gi) for gi in range(groups)]
        vv = [load(v_ref, u) for u in us]
        AR = [_dot(lhs[u], bd(H[gi])) for gi, u in enumerate(us)]
        AV = [_dot(AA[u], bd(vv[gi])) for gi, u in enumerate(us)]
        U = [_dot(X[u].astype(BF16), bd(AR[gi][0:C] + AV[gi][0:C])) for gi, u in enumerate(us)]
        for gi, u in enumerate(us):
            y_ref[0, c * C:(c + 1) * C, gi * W:(gi + 1) * W] = (
                AR[gi][C:2 * C] + AV[gi][C:2 * C] + _dot(A_rb[u], bd(U[gi])))
        H = [gam[u] * H[gi]
             + fold(_dot_tn(lhs_t[u], jnp.concatenate([U[gi], vv[gi]], axis=0).astype(BF16)))
             for gi, u in enumerate(us)]
    for gi in range(groups):
        h_sc[gi] = H[gi]


def _rw_out_kernel(y0_ref, y1_ref, bonus_ref, g_ref, x_ref, gng_ref, gnb_ref, wo_ref, e_ref, et_ref,
                   lg_ref, lb_ref, o_ref, *, alpha):
    y = y0_ref[0] + y1_ref[0]
    inv_n = 1.0 / RW_HEAD
    mu = _seg_bcast(_seg_sum(y, e_ref) * inv_n, et_ref)
    yc = y - mu
    var = _seg_sum(yc * yc, e_ref) * inv_n
    yn = yc * _seg_bcast(lax.rsqrt(var + RW_GN_EPS), et_ref) * gng_ref[...] + gnb_ref[...]
    yo = (yn + bonus_ref[0]) * g_ref[0]
    m = _dot(yo.astype(BF16), wo_ref[...])
    o_ref[0] = _layer_norm(alpha * x_ref[0] + m, lg_ref[...], lb_ref[...])


def _rwkv_layer(h, mu, w_rkv, w0, w1, w2, a0, a1, a2, g1, g2, k_k, k_a, r_k, gn_g, gn_b, w_o,
                lg, lb, pad, alpha):
    B, Tp, D = h.shape
    heads = D // RW_HEAD

    def pad_cols(w, n):
        return jnp.concatenate([w, jnp.zeros(w.shape[:-1] + (n - w.shape[-1],), w.dtype)], axis=-1)

    def pad_rows(w, n):
        return jnp.concatenate([w, jnp.zeros(w.shape[:-2] + (n - w.shape[-2], w.shape[-1]), w.dtype)], axis=-2)

    g1p = pad_cols(g1, 256).astype(BF16)
    g2p = pad_rows(g2, 256).astype(BF16)
    w1p = jnp.concatenate([pad_cols(w1[0], 128), pad_cols(w1[1], 128)], axis=1).astype(BF16)
    a1p = jnp.concatenate([pad_cols(a1[0], 128), pad_cols(a1[1], 128)], axis=1).astype(BF16)
    w2p = pad_rows(w2, 128).astype(BF16)
    a2p = pad_rows(a2, 128).astype(BF16)
    e = (jnp.arange(D)[:, None] // RW_HEAD == jnp.arange(128)[None, :]).astype(BF16)
    et = e.T

    tm = _row_tile(Tp, 256)
    n = Tp // tm
    prev, nxt = _halo_specs(tm, D, Tp)
    vec = _const_spec((1, D))
    outs = pl.pallas_call(
        functools.partial(_rw_proj_kernel, tm=tm, pad=pad, n_tiles=n),
        grid=(B, n),
        in_specs=[_row_spec(tm, D), prev, nxt, _const_spec((6, D)),
                  _const_spec((D, D)), _const_spec((D, D)), _const_spec((D, D)),
                  _const_spec(g1p.shape), _const_spec(g2p.shape),
                  _const_spec(w1p.shape), _const_spec(w2p.shape), _const_spec((2, D)),
                  _const_spec(a1p.shape), _const_spec(a2p.shape), _const_spec((2, D)),
                  vec, vec, vec, _const_spec(e.shape), _const_spec(et.shape)],
        out_specs=[_row_spec(tm, D)] * 11,
        out_shape=[jax.ShapeDtypeStruct((B, Tp, D), F32)] * 11,
        compiler_params=_params(),
        name="rw_proj",
    )(h, h, h, mu, w_rkv[0].astype(BF16), w_rkv[1].astype(BF16), w_rkv[2].astype(BF16), g1p, g2p,
      w1p, w2p, w0, a1p, a2p, a0, k_k.reshape(1, D), k_a.reshape(1, D), r_k.reshape(1, D), e, et)
    r, v, kk, g, bonus, lw0, lw1, b0, b1, kd0, kd1 = outs

    ts = _row_tile(Tp, 256)
    ns = Tp // ts
    ys = []
    for rev, lw, bb, kd in ((False, lw0, b0, kd0), (True, lw1, b1, kd1)):
        if rev:
            spec = pl.BlockSpec((1, ts, D), lambda b, i: (b, ns - 1 - i, 0))
        else:
            spec = _row_spec(ts, D)
        ys.append(pl.pallas_call(
            functools.partial(_rw_scan_kernel, rev=rev, n_sub=ts // RW_CHUNK),
            grid=(B, ns),
            in_specs=[spec] * 6,
            out_specs=spec,
            out_shape=jax.ShapeDtypeStruct((B, Tp, D), F32),
            scratch_shapes=[pltpu.VMEM((D // (RW_GROUP * RW_HEAD), RW_HEAD, RW_GROUP * RW_HEAD), F32)],
            compiler_params=pltpu.CompilerParams(
                dimension_semantics=("parallel", "arbitrary"), vmem_limit_bytes=VMEM_LIMIT),
            name="rw_scan_bwd" if rev else "rw_scan_fwd",
        )(r, lw, kk, bb, kd, v))

    tm = _row_tile(Tp, 640)
    return pl.pallas_call(
        functools.partial(_rw_out_kernel, alpha=alpha),
        grid=(B, Tp // tm),
        in_specs=[_row_spec(tm, D)] * 5 + [vec, vec, _const_spec((D, D)), _const_spec(e.shape),
                                            _const_spec(et.shape), vec, vec],
        out_specs=_row_spec(tm, D),
        out_shape=jax.ShapeDtypeStruct((B, Tp, D), F32),
        compiler_params=_params(),
        name="rw_out",
    )(ys[0], ys[1], bonus, g, h, gn_g.reshape(1, D), gn_b.reshape(1, D), w_o.astype(BF16), e, et,
      lg.reshape(1, D), lb.reshape(1, D))


def _gelu_tanh(z):
    return 0.5 * z * (1.0 + jnp.tanh(math.sqrt(2.0 / math.pi) * (z + 0.044715 * z * z * z)))


def _lru_proj_kernel(x_ref, wg_ref, wu_ref, gate_out, u_out, *, tm, pad):
    i = pl.program_id(1)
    xb = x_ref[0].astype(BF16)
    gate_out[0] = _gelu_tanh(_dot(xb, wg_ref[...]))
    row = i * tm + lax.broadcasted_iota(jnp.int32, (tm, 1), 0)
    u_out[0] = jnp.where(row >= pad, _dot(xb, wu_ref[...]), 0.0)


def _lru_scan_kernel(u_ref, up_ref, un_ref, cw_ref, cb_ref, gw_ref, gb_ref, lam_ref, h_out,
                     a_sc, b_sc, h_sc, *, tm, pad, n_tiles, rev):
    step = pl.program_id(1)
    i = (n_tiles - 1 - step) if rev else step

    @pl.when(step == 0)
    def _():
        h_sc[...] = jnp.zeros_like(h_sc)

    ext = _ext_rows(u_ref[0], up_ref[0], un_ref[0], i, n_tiles, tm, 0)
    uc = cb_ref[...]
    for kt in range(cw_ref.shape[0]):
        uc = uc + cw_ref[kt:kt + 1, :] * _shift(ext, kt - 2, tm)
    width = uc.shape[1]
    nblk = width // LRU_BLOCK
    sp = _softplus(-lam_ref[...])
    row = i * tm + lax.broadcasted_iota(jnp.int32, (tm, 1), 0)
    ok = row >= pad
    for nb in range(nblk):
        cs = slice(nb * LRU_BLOCK, (nb + 1) * LRU_BLOCK)
        ub = uc[:, cs]
        ubb = ub.astype(BF16)
        rg = _sigmoid(_dot(ubb, gw_ref[0, nb]) + gb_ref[0:1, cs])
        ig = _sigmoid(_dot(ubb, gw_ref[1, nb]) + gb_ref[1:2, cs])
        a = jnp.exp(-LRU_C * rg * sp[:, cs])
        a_sc[:, cs] = a
        b_sc[:, cs] = jnp.where(ok, jnp.sqrt(1.0 - a * a) * (ig * ub), 0.0)

    def body(s, hcur):
        t = (tm - 1 - s) if rev else s
        hnew = a_sc[pl.ds(t, 1), :] * hcur + b_sc[pl.ds(t, 1), :]
        h_out[0, pl.ds(t, 1), :] = hnew
        return hnew

    h_sc[...] = lax.fori_loop(0, tm, body, h_sc[...], unroll=8)


def _lru_out_kernel(h0_ref, h1_ref, gate_ref, x_ref, wo_ref, lg_ref, lb_ref, o_ref, *, alpha):
    hh = (h0_ref[0] + h1_ref[0]) * gate_ref[0]
    m = _dot(hh.astype(BF16), wo_ref[...])
    o_ref[0] = _layer_norm(alpha * x_ref[0] + m, lg_ref[...], lb_ref[...])


def _lru_layer(h, w_in, conv_w, conv_b, gate_w, gate_b, lam, w_o, lg, lb, pad, alpha):
    B, Tp, D = h.shape
    width = w_o.shape[0]
    tm = _row_tile(Tp, 640)
    n = Tp // tm
    gate, u = pl.pallas_call(
        functools.partial(_lru_proj_kernel, tm=tm, pad=pad),
        grid=(B, n),
        in_specs=[_row_spec(tm, D), _const_spec((D, width)), _const_spec((D, width))],
        out_specs=[_row_spec(tm, width)] * 2,
        out_shape=[jax.ShapeDtypeStruct((B, Tp, width), F32)] * 2,
        compiler_params=_params(),
        name="lru_proj",
    )(h, w_in[:, :width].astype(BF16), w_in[:, width:].astype(BF16))

    ts = _row_tile(Tp, 256)
    ns = Tp // ts
    nb8 = ts // HALO
    last = Tp // HALO - 1
    hs = []
    for d, rev in ((0, False), (1, True)):
        if rev:
            cur = pl.BlockSpec((1, ts, width), lambda b, s: (b, ns - 1 - s, 0))
            prev = pl.BlockSpec((1, HALO, width), lambda b, s: (b, jnp.maximum((ns - 1 - s) * nb8 - 1, 0), 0))
            nxt = pl.BlockSpec((1, HALO, width), lambda b, s: (b, jnp.minimum((ns - s) * nb8, last), 0))
        else:
            cur = _row_spec(ts, width)
            prev, nxt = _halo_specs(ts, width, Tp)
        hs.append(pl.pallas_call(
            functools.partial(_lru_scan_kernel, tm=ts, pad=pad, n_tiles=ns, rev=rev),
            grid=(B, ns),
            in_specs=[cur, prev, nxt, _const_spec(conv_w.shape), _const_spec((1, width)),
                      _const_spec(gate_w.shape[1:]), _const_spec((2, width)), _const_spec((1, width))],
            out_specs=cur,
            out_shape=jax.ShapeDtypeStruct((B, Tp, width), F32),
            scratch_shapes=[pltpu.VMEM((ts, width), F32), pltpu.VMEM((ts, width), F32),
                            pltpu.VMEM((1, width), F32)],
            compiler_params=pltpu.CompilerParams(
                dimension_semantics=("parallel", "arbitrary"), vmem_limit_bytes=VMEM_LIMIT),
            name="lru_scan_bwd" if rev else "lru_scan_fwd",
        )(u, u, u, conv_w, conv_b.reshape(1, width), gate_w[d].astype(BF16), gate_b[d],
          lam[d].reshape(1, width)))

    return pl.pallas_call(
        functools.partial(_lru_out_kernel, alpha=alpha),
        grid=(B, n),
        in_specs=[_row_spec(tm, width)] * 3 + [_row_spec(tm, D), _const_spec((width, D)),
                                               _const_spec((1, D)), _const_spec((1, D))],
        out_specs=_row_spec(tm, D),
        out_shape=jax.ShapeDtypeStruct((B, Tp, D), F32),
        compiler_params=_params(),
        name="lru_out",
    )(hs[0], hs[1], gate, h, w_o.astype(BF16), lg.reshape(1, D), lb.reshape(1, D))


def kernel(x, positions, meta_tokens, ln_g, ln_b, ffn_w_in, ffn_conv_w, ffn_conv_b, ffn_w_out, mla_w_in, mla_q_norm, mla_w_q_up, mla_kv_norm, mla_w_kv_up, mla_w_o, rw_mu, rw_w_rkv, rw_w0, rw_w1, rw_w2, rw_a0, rw_a1, rw_a2, rw_g1, rw_g2, rw_k_k, rw_k_a, rw_r_k, rw_gn_g, rw_gn_b, rw_w_o, lru_w_in, lru_conv_w, lru_conv_b, lru_gate_w, lru_gate_b, lru_lambda, lru_w_o):
    B, S, D = x.shape
    depth = ln_g.shape[0]
    T = S + N_META
    Tp = -(-T // SEQ_ALIGN) * SEQ_ALIGN
    pad = Tp - T
    alpha = (2.0 * depth) ** 0.25
    dt = x.dtype
    h = jnp.concatenate([jnp.zeros((B, pad, D), dt),
                         jnp.broadcast_to(meta_tokens[None].astype(dt), (B, N_META, D)), x], axis=1)
    pos = jnp.concatenate(
        [jnp.zeros((B, pad), jnp.int32),
         jnp.broadcast_to(jnp.arange(N_META, dtype=jnp.int32)[None, :], (B, N_META)),
         positions + N_META], axis=1)
    inv_freq = ROPE_BASE ** (-jnp.arange(0, MLA_ROPE, 2, dtype=F32) / MLA_ROPE)
    ang = pos.astype(F32)[..., None] * inv_freq
    zeros = jnp.zeros((B, Tp, 128 - MLA_ROPE), F32)
    cos = jnp.cos(ang)
    sin = jnp.sin(ang)
    cos128 = jnp.concatenate([cos, cos, zeros], axis=-1)
    sin128 = jnp.concatenate([sin, sin, zeros], axis=-1)
    for i in range(depth):
        kind = i % 3
        j = i // 3
        lg, lb = ln_g[i, 0], ln_b[i, 0]
        if kind == 0:
            h = _mla_layer(h, cos128, sin128, mla_w_in[j], mla_q_norm[j], mla_w_q_up[j], mla_kv_norm[j],
                           mla_w_kv_up[j], mla_w_o[j], lg, lb, pad, alpha)
        elif kind == 1:
            h = _rwkv_layer(h, rw_mu[j], rw_w_rkv[j], rw_w0[j], rw_w1[j], rw_w2[j], rw_a0[j], rw_a1[j],
                            rw_a2[j], rw_g1[j], rw_g2[j], rw_k_k[j], rw_k_a[j], rw_r_k[j], rw_gn_g[j],
                            rw_gn_b[j], rw_w_o[j], lg, lb, pad, alpha)
        else:
            h = _lru_layer(h, lru_w_in[j], lru_conv_w[j], lru_conv_b[j], lru_gate_w[j], lru_gate_b[j],
                           lru_lambda[j], lru_w_o[j], lg, lb, pad, alpha)
        h = _ffn_layer(h, ffn_w_in[i], ffn_conv_w[i], ffn_conv_b[i], ffn_w_out[i],
                       ln_g[i, 1], ln_b[i, 1], pad, alpha)
    return h[:, pad + N_META:]
```

```python
import functools
import math

import jax
import jax.numpy as jnp
from jax import lax
from jax.experimental import pallas as pl
from jax.experimental.pallas import tpu as pltpu

F32 = jnp.float32
BF16 = jnp.bfloat16

N_META = 16
LN_EPS = 1e-5
RMS_EPS = 1e-6
ROPE_BASE = 10000.0
MLA_NOPE = 128
MLA_ROPE = 64
MLA_V = 128
MLA_Q_RANK = 256
MLA_KV_RANK = 128
RW_HEAD = 64
RW_GN_EPS = 64e-5
LRU_C = 8.0
LRU_BLOCK = 256
SEQ_ALIGN = 256
HALO = 8
RW_CHUNK = 64
RW_GROUP = 4
MASK_NEG = -1e30
MASK_LANE = 128 + MLA_ROPE
ONES_LANE = MASK_LANE + 1
FLASH_UNROLL = 4
FLASH_BIG = 5
VMEM_LIMIT = 56 * 1024 * 1024


def _row_tile(tp, target):
    best = 128
    for t in range(128, min(tp, target) + 1, 128):
        if tp % t == 0:
            best = t
    return best


def _const_spec(shape):
    nd = len(shape)
    return pl.BlockSpec(shape, lambda *_: (0,) * nd, pipeline_mode=pl.Buffered(1))


def _row_spec(tm, c):
    return pl.BlockSpec((1, tm, c), lambda b, i: (b, i, 0))


def _halo_specs(tm, c, tp):
    nb = tm // HALO
    last = tp // HALO - 1
    prev = pl.BlockSpec((1, HALO, c), lambda b, i: (b, jnp.maximum(i * nb - 1, 0), 0))
    nxt = pl.BlockSpec((1, HALO, c), lambda b, i: (b, jnp.minimum((i + 1) * nb, last), 0))
    return prev, nxt


def _params(n_parallel=2):
    return pltpu.CompilerParams(
        dimension_semantics=("parallel",) * n_parallel,
        vmem_limit_bytes=VMEM_LIMIT)


def _dot(a, b):
    return jnp.dot(a, b, preferred_element_type=F32)


def _dot_nt(a, b):
    return lax.dot_general(a, b, (((1,), (1,)), ((), ())), preferred_element_type=F32)


def _dot_tn(a, b):
    return lax.dot_general(a, b, (((0,), (0,)), ((), ())), preferred_element_type=F32)


def _layer_norm(z, g, b):
    mu = jnp.mean(z, axis=-1, keepdims=True)
    zc = z - mu
    var = jnp.mean(zc * zc, axis=-1, keepdims=True)
    return zc * lax.rsqrt(var + LN_EPS) * g + b


def _sigmoid(z):
    return 1.0 / (1.0 + jnp.exp(-z))


def _softplus(z):
    return jnp.maximum(z, 0.0) + jnp.log(1.0 + jnp.exp(-jnp.abs(z)))


def _split2(z):
    hi = z.astype(BF16)
    lo = (z - hi.astype(F32)).astype(BF16)
    return hi, lo


def _ext_rows(x, prev, nxt, i, n_tiles, tm, pad):
    ext = jnp.concatenate([prev, x, nxt], axis=0)
    row = i * tm - HALO + lax.broadcasted_iota(jnp.int32, (tm + 2 * HALO, 1), 0)
    ok = (row >= pad) & (row < n_tiles * tm)
    return jnp.where(ok, ext, 0.0)


def _shift(ext, k, tm):
    n = ext.shape[0]
    if k == 0:
        return ext[HALO:HALO + tm]
    return pltpu.roll(ext, (-k) % n, 0)[HALO:HALO + tm]


def _mla_proj_kernel(x_ref, cos_ref, sin_ref, w_in_ref, qn_ref, kvn_ref, wq_ref, wuk_ref,
                     q_ref, k_ref, *, tm, pad, heads, qscale):
    i = pl.program_id(1)
    x = x_ref[0]
    hp = _dot(x.astype(BF16), w_in_ref[...])
    cq = hp[:, :MLA_Q_RANK]
    ckv = hp[:, MLA_Q_RANK:MLA_Q_RANK + MLA_KV_RANK]
    kpe = hp[:, 384:512]
    kpr = hp[:, 512:640]
    cq = cq * lax.rsqrt(jnp.mean(cq * cq, axis=-1, keepdims=True) + RMS_EPS) * qn_ref[...]
    ckv = ckv * lax.rsqrt(jnp.mean(ckv * ckv, axis=-1, keepdims=True) + RMS_EPS) * kvn_ref[...]
    cos = cos_ref[0]
    sin = sin_ref[0]
    lane = lax.broadcasted_iota(jnp.int32, (tm, 128), 1)
    row = i * tm + lax.broadcasted_iota(jnp.int32, (tm, 128), 0)
    kslab = kpe * cos + kpr * sin
    kslab = jnp.where(lane == MASK_LANE - 128, jnp.where(row < pad, MASK_NEG, 0.0), kslab)
    kslab = jnp.where(lane == ONES_LANE - 128, 1.0, kslab)
    k_ref[0, :, 0:128] = ckv.astype(BF16)
    k_ref[0, :, 128:256] = kslab.astype(BF16)
    q = _dot(cq.astype(BF16), wq_ref[...])
    hw = heads * 128
    for h in range(heads):
        qn = q[:, h * 128:(h + 1) * 128]
        ql = _dot(qn.astype(BF16), wuk_ref[h])
        qp = q[:, hw + h * 128:hw + (h + 1) * 128] * cos + q[:, 2 * hw + h * 128:2 * hw + (h + 1) * 128] * sin
        qp = jnp.where(lane == MASK_LANE - 128, 1.0, qp * qscale)
        q_ref[0, h, :, 0:128] = (ql * qscale).astype(BF16)
        q_ref[0, h, :, 128:256] = qp.astype(BF16)


def _flash_plan(n_tiles):
    big = FLASH_BIG if n_tiles > FLASH_BIG else 1
    n_big, last = divmod(n_tiles - 1, big)
    return big, n_big, last


def _flash_kernel(q_ref, k_ref, o_ref, *scratch, tq, heads, n_tiles, pad):
    all_pad = (pl.program_id(1) + 1) * tq <= pad

    @pl.when(all_pad)
    def _():
        o_ref[...] = jnp.zeros_like(o_ref)

    @pl.when(jnp.logical_not(all_pad))
    def _():
        _flash_block(q_ref, k_ref, o_ref, *scratch, tq=tq, heads=heads, n_tiles=n_tiles)


def _flash_block(q_ref, k_ref, o_ref, m_sc, acc_sc, sa_sc, sb_sc, mca_sc, mcb_sc, *, tq, heads, n_tiles):
    rows = heads * tq
    big, n_big, last = _flash_plan(n_tiles)
    n_chunks = 1 + n_big + (1 if last else 0)
    q = q_ref[0].reshape(rows, 256)
    m_sc[...] = jnp.full((rows, 128), -jnp.inf, F32)
    acc_sc[...] = jnp.zeros((rows, 256), F32)
    slots = ((sa_sc, mca_sc), (sb_sc, mcb_sc))

    def big_start(k):
        return 256 + (k - 1) * (big * 256)

    def chunk(k):
        if k == 0:
            return 0, 1
        if k <= n_big:
            return big_start(k), big
        return big_start(n_big + 1), last

    def scores(start, n_t, slot):
        s_ref, mc_ref = slot
        kc = k_ref[0, pl.ds(start, n_t * 256), :]
        mc = None
        for t in range(n_t):
            st = _dot_nt(q, kc[t * 256:(t + 1) * 256])
            s_ref[t] = st
            mt = jnp.maximum(st[:, 0:128], st[:, 128:256])
            mc = mt if mc is None else jnp.maximum(mc, mt)
        mc_ref[...] = mc

    def softmax_values(start, n_t, slot):
        s_ref, mc_ref = slot
        kc = k_ref[0, pl.ds(start, n_t * 256), :]
        m_old = m_sc[...]
        m_cur = jnp.max(mc_ref[...], axis=-1, keepdims=True)
        m_new = jnp.maximum(m_old, jnp.broadcast_to(m_cur, (rows, 128)))
        alpha = jnp.exp2(m_old - m_new)
        m2 = jnp.tile(m_new, (1, 2))
        p = jnp.concatenate([jnp.exp2((s_ref[t] - m2).astype(BF16)) for t in range(n_t)], axis=1)
        acc_sc[...] = jnp.tile(alpha, (1, 2)) * acc_sc[...] + _dot(p, kc)
        m_sc[...] = m_new

    def static_step(k):
        if k + 1 < n_chunks:
            scores(*chunk(k + 1), slots[(k + 1) % 2])
        softmax_values(*chunk(k), slots[k % 2])

    scores(*chunk(0), slots[0])
    static_step(0)
    n_loop = max(n_big - 1, 0) // FLASH_UNROLL

    def body(i, carry):
        k0 = 1 + FLASH_UNROLL * i
        for r in range(FLASH_UNROLL):
            nxt = pl.multiple_of(big_start(k0 + r + 1), 256)
            cur = pl.multiple_of(big_start(k0 + r), 256)
            scores(nxt, big, slots[r % 2])
            softmax_values(cur, big, slots[(r + 1) % 2])
        return carry

    lax.fori_loop(0, n_loop, body, 0)
    for k in range(1 + FLASH_UNROLL * n_loop, n_chunks):
        static_step(k)
    acc = acc_sc[...]
    out = acc[:, 0:128] / acc[:, ONES_LANE:ONES_LANE + 1]
    for h in range(heads):
        o_ref[0, :, h * 128:(h + 1) * 128] = out[h * tq:(h + 1) * tq].astype(o_ref.dtype)


def _mla_out_kernel(ol_ref, x_ref, wuv_ref, wo_ref, g_ref, b_ref, o_ref, *, heads, alpha):
    ol = ol_ref[0]
    parts = [_dot(ol[:, h * 128:(h + 1) * 128], wuv_ref[h]).astype(BF16) for h in range(heads)]
    o = jnp.concatenate(parts, axis=-1)
    m = _dot(o, wo_ref[...])
    o_ref[0] = _layer_norm(alpha * x_ref[0] + m, g_ref[...], b_ref[...])


def _mla_layer(h, cos128, sin128, w_in, q_norm, w_q_up, kv_norm, w_kv_up, w_o, lg, lb, pad, alpha):
    B, Tp, D = h.shape
    heads = w_kv_up.shape[1]
    hw = heads * 128

    def rot(w):
        return jnp.concatenate([-w[..., MLA_ROPE // 2:], w[..., :MLA_ROPE // 2]], axis=-1)

    def slab(w):
        return jnp.concatenate([w, jnp.zeros(w.shape[:-1] + (128 - MLA_ROPE,), w.dtype)], axis=-1)

    kpe_w = w_in[:, MLA_Q_RANK + MLA_KV_RANK:]
    w_in_aug = jnp.concatenate(
        [w_in[:, :MLA_Q_RANK + MLA_KV_RANK], slab(kpe_w), slab(rot(kpe_w))], axis=1).astype(BF16)
    wq3 = w_q_up.reshape(MLA_Q_RANK, heads, MLA_NOPE + MLA_ROPE)
    wq_pe = wq3[:, :, MLA_NOPE:]
    wq_all = jnp.concatenate(
        [wq3[:, :, :MLA_NOPE].reshape(MLA_Q_RANK, hw),
         slab(wq_pe).reshape(MLA_Q_RANK, hw),
         slab(rot(wq_pe)).reshape(MLA_Q_RANK, hw)], axis=1).astype(BF16)
    w_ukT = jnp.transpose(w_kv_up[:, :, :MLA_NOPE], (1, 2, 0)).astype(BF16)
    w_uv = jnp.transpose(w_kv_up[:, :, MLA_NOPE:], (1, 0, 2)).astype(BF16)
    qscale = (MLA_NOPE + MLA_ROPE) ** -0.5 * math.log2(math.e)

    tm = _row_tile(Tp, 256)
    n = Tp // tm
    q, k = pl.pallas_call(
        functools.partial(_mla_proj_kernel, tm=tm, pad=pad, heads=heads, qscale=qscale),
        grid=(B, n),
        in_specs=[_row_spec(tm, D), _row_spec(tm, 128), _row_spec(tm, 128),
                  _const_spec(w_in_aug.shape), _const_spec((1, MLA_Q_RANK)), _const_spec((1, MLA_KV_RANK)),
                  _const_spec(wq_all.shape), _const_spec(w_ukT.shape)],
        out_specs=[pl.BlockSpec((1, heads, tm, 256), lambda b, i: (b, 0, i, 0)), _row_spec(tm, 256)],
        out_shape=[jax.ShapeDtypeStruct((B, heads, Tp, 256), BF16),
                   jax.ShapeDtypeStruct((B, Tp, 256), BF16)],
        compiler_params=_params(),
        name="mla_proj",
    )(h, cos128, sin128, w_in_aug, q_norm.reshape(1, -1), kv_norm.reshape(1, -1), wq_all, w_ukT)

    tq = 64
    n_tiles = Tp // 256
    big = _flash_plan(n_tiles)[0]
    rows = heads * tq
    o_lat = pl.pallas_call(
        functools.partial(_flash_kernel, tq=tq, heads=heads, n_tiles=n_tiles, pad=pad),
        grid=(B, Tp // tq),
        in_specs=[pl.BlockSpec((1, heads, tq, 256), lambda b, i: (b, 0, i, 0)),
                  pl.BlockSpec((1, Tp, 256), lambda b, i: (b, 0, 0))],
        out_specs=_row_spec(tq, hw),
        out_shape=jax.ShapeDtypeStruct((B, Tp, hw), BF16),
        scratch_shapes=[pltpu.VMEM((rows, 128), F32), pltpu.VMEM((rows, 256), F32),
                        pltpu.VMEM((big, rows, 256), F32), pltpu.VMEM((big, rows, 256), F32),
                        pltpu.VMEM((rows, 128), F32), pltpu.VMEM((rows, 128), F32)],
        compiler_params=_params(),
        name="mla_flash",
    )(q, k)

    tm = _row_tile(Tp, 640)
    return pl.pallas_call(
        functools.partial(_mla_out_kernel, heads=heads, alpha=alpha),
        grid=(B, Tp // tm),
        in_specs=[_row_spec(tm, hw), _row_spec(tm, D), _const_spec(w_uv.shape),
                  _const_spec(w_o.shape), _const_spec((1, D)), _const_spec((1, D))],
        out_specs=_row_spec(tm, D),
        out_shape=jax.ShapeDtypeStruct((B, Tp, D), F32),
        compiler_params=_params(),
        name="mla_out",
    )(o_lat, h, w_uv, w_o.astype(BF16), lg.reshape(1, D), lb.reshape(1, D))


def _ffn_kernel(x_ref, xp_ref, xn_ref, wg_ref, wu_ref, cw_ref, cb_ref, wo_ref, g_ref, b_ref, o_ref,
                *, tm, pad, n_tiles, ff_chunk, alpha):
    i = pl.program_id(1)
    x = x_ref[0]
    xb = x.astype(BF16)
    ext = jnp.concatenate([xp_ref[0], x, xn_ref[0]], axis=0).astype(BF16)
    n_ext = tm + 2 * HALO
    row = i * tm - HALO + lax.broadcasted_iota(jnp.int32, (n_ext, 1), 0)
    ok = (row >= pad) & (row < n_tiles * tm)
    d_ff = wg_ref.shape[1]
    acc = jnp.zeros((tm, x.shape[1]), F32)
    for j in range(d_ff // ff_chunk):
        cs = slice(j * ff_chunk, (j + 1) * ff_chunk)
        g = jnp.where(ok, _dot(ext, wg_ref[:, cs]), 0.0)
        u = _dot(xb, wu_ref[:, cs])
        gc = (cw_ref[0:1, cs] * _shift(g, -1, tm) + cw_ref[1:2, cs] * _shift(g, 0, tm)
              + cw_ref[2:3, cs] * _shift(g, 1, tm) + cb_ref[:, cs])
        act = gc * _sigmoid(gc) * u
        acc = acc + _dot(act.astype(BF16), wo_ref[cs, :])
    o_ref[0] = _layer_norm(alpha * x + acc, g_ref[...], b_ref[...])


def _ffn_layer(h, w_in, conv_w, conv_b, w_out, lg, lb, pad, alpha):
    B, Tp, D = h.shape
    d_ff = w_out.shape[0]
    tm = _row_tile(Tp, 640)
    n = Tp // tm
    prev, nxt = _halo_specs(tm, D, Tp)
    wg = w_in[:, :d_ff].astype(BF16)
    wu = w_in[:, d_ff:].astype(BF16)
    return pl.pallas_call(
        functools.partial(_ffn_kernel, tm=tm, pad=pad, n_tiles=n, ff_chunk=256, alpha=alpha),
        grid=(B, n),
        in_specs=[_row_spec(tm, D), prev, nxt, _const_spec(wg.shape), _const_spec(wu.shape),
                  _const_spec(conv_w.shape), _const_spec((1, d_ff)), _const_spec(w_out.shape),
                  _const_spec((1, D)), _const_spec((1, D))],
        out_specs=_row_spec(tm, D),
        out_shape=jax.ShapeDtypeStruct((B, Tp, D), F32),
        compiler_params=_params(),
        name="ffn",
    )(h, h, h, wg, wu, conv_w, conv_b.reshape(1, d_ff), w_out.astype(BF16),
      lg.reshape(1, D), lb.reshape(1, D))


def _seg_sum(z, e_ref):
    hi, lo = _split2(z)
    return _dot(hi, e_ref[...]) + _dot(lo, e_ref[...])


def _seg_bcast(zs, et_ref):
    hi, lo = _split2(zs)
    return _dot(hi, et_ref[...]) + _dot(lo, et_ref[...])


def _rw_proj_kernel(x_ref, xp_ref, xn_ref, mu_ref, wr_ref, wk_ref, wv_ref, g1_ref, g2_ref,
                    w1_ref, w2_ref, w0_ref, a1_ref, a2_ref, a0_ref, kk_ref, ka_ref, rk_ref,
                    e_ref, et_ref,
                    r_out, v_out, kk_out, g_out, bonus_out, lw0_out, lw1_out, b0_out, b1_out,
                    kd0_out, kd1_out, *, tm, pad, n_tiles):
    i = pl.program_id(1)
    ext = _ext_rows(x_ref[0], xp_ref[0], xn_ref[0], i, n_tiles, tm, pad)
    x = _shift(ext, 0, tm)
    xx = 0.5 * (_shift(ext, -1, tm) + _shift(ext, 1, tm)) - x
    mix = [(x + xx * mu_ref[c:c + 1, :]).astype(BF16) for c in range(6)]
    xr, xw, xk, xv, xa, xg = mix
    row = i * tm + lax.broadcasted_iota(jnp.int32, (tm, 1), 0)
    ok = row >= pad
    r = _dot(xr, wr_ref[...])
    k = jnp.where(ok, _dot(xk, wk_ref[...]), 0.0)
    v = jnp.where(ok, _dot(xv, wv_ref[...]), 0.0)
    g = _dot(_sigmoid(_dot(xg, g1_ref[...])).astype(BF16), g2_ref[...])
    tw = jnp.tanh(_dot(xw, w1_ref[...])).astype(BF16)
    ta = _dot(xa, a1_ref[...]).astype(BF16)
    kk = k * kk_ref[...]
    ss = _seg_sum(kk * kk, e_ref)
    kk = kk * _seg_bcast(lax.rsqrt(jnp.maximum(ss, 1e-24)), et_ref)
    r_out[0] = r.astype(r_out.dtype)
    v_out[0] = v.astype(v_out.dtype)
    kk_out[0] = kk.astype(kk_out.dtype)
    g_out[0] = g
    kd_sum = None
    for d, (lw_out, b_out, kd_out) in enumerate(((lw0_out, b0_out, kd0_out), (lw1_out, b1_out, kd1_out))):
        wl = w0_ref[d:d + 1, :] + _dot(tw[:, d * 128:(d + 1) * 128], w2_ref[d])
        w_log = -_softplus(-wl) - 0.5
        lw_out[0] = -jnp.exp(w_log)
        a = _sigmoid(a0_ref[d:d + 1, :] + _dot(ta[:, d * 128:(d + 1) * 128], a2_ref[d]))
        kd = k * (1.0 + (a - 1.0) * ka_ref[...])
        kd_out[0] = kd.astype(kd_out.dtype)
        b_out[0] = (kk * a).astype(b_out.dtype)
        kd_sum = kd if kd_sum is None else kd_sum + kd
    bs = _seg_sum(r * kd_sum * rk_ref[...], e_ref)
    bonus_out[0] = _seg_bcast(bs, et_ref) * v


def _rw_scan_kernel(r_ref, lw_ref, kk_ref, b_ref, kd_ref, v_ref, y_ref, h_sc, *, rev, n_sub):
    C = RW_CHUNK
    W = RW_GROUP * RW_HEAD
    groups = r_ref.shape[2] // W

    @pl.when(pl.program_id(1) == 0)
    def _():
        h_sc[...] = jnp.zeros_like(h_sc)

    ti = lax.broadcasted_iota(jnp.int32, (C, W), 0)
    si = lax.broadcasted_iota(jnp.int32, (C, W), 1) % RW_HEAD
    if rev:
        before = si > ti
    else:
        before = si < ti
    incl = before | (si == ti)
    eye = (si == ti).astype(F32)
    lvl_masks = []
    m = 1
    while m < C:
        same = (ti // (2 * m)) == (si // (2 * m))
        t_hi = (ti // m) % 2 == 1
        s_hi = (si // m) % 2 == 1
        if rev:
            lvl_masks.append(same & (~t_hi) & s_hi)
        else:
            lvl_masks.append(same & t_hi & (~s_hi))
        m *= 2
    bi = lax.broadcasted_iota(jnp.int32, (W, W), 0) // RW_HEAD
    bj = lax.broadcasted_iota(jnp.int32, (W, W), 1) // RW_HEAD
    bd_mask = bi == bj
    ones_bd = bd_mask.astype(BF16)
    ci =lax.broadcasted_iota(jnp.int32, (C, C), 0)
    cj = lax.broadcasted_iota(jnp.int32, (C, C), 1)
    tri = ((cj >= ci) if rev else (cj <= ci)).astype(BF16)

    def bd(z):
        zb = z.astype(BF16)
        return jnp.concatenate([zb] * RW_GROUP, axis=0) * ones_bd

    def fold(z):
        zm = jnp.where(bd_mask, z, 0.0)
        out = zm[0:RW_HEAD]
        for a in range(1, RW_GROUP):
            out = out + zm[a * RW_HEAD:(a + 1) * RW_HEAD]
        return out

    order = list(range(n_sub - 1, -1, -1)) if rev else list(range(n_sub))
    units = [(c, gi) for c in order for gi in range(groups)]

    def load(ref, u):
        c, gi = u
        return ref[0, c * C:(c + 1) * C, gi * W:(gi + 1) * W].astype(F32)

    def split3(z):
        z1 = z.astype(BF16)
        rem = z - z1.astype(F32)
        z2 = rem.astype(BF16)
        z3 = (rem - z2.astype(F32)).astype(BF16)
        return z1, z2, z3

    gsum, gtot = {}, {}
    for u in units:
        l1, l2, l3 = split3(load(lw_ref, u))
        gs = _dot(tri, l1) + _dot(tri, l2) + _dot(tri, l3)
        gsum[u] = gs
        gtot[u] = gs[0:1] if rev else gs[C - 1:C]
    lhs, Bt, Kt, lhs_t, gam = {}, {}, {}, {}, {}
    for u in units:
        gs = gsum[u]
        bb = load(b_ref, u)
        kd = load(kd_ref, u)
        en = jnp.exp(-gs)
        ec = jnp.exp(gtot[u] - gs)
        At = -load(kk_ref, u) * jnp.exp(gs - load(lw_ref, u))
        Rt = load(r_ref, u) * jnp.exp(gs)
        lhs[u] = jnp.concatenate([At, Rt], axis=0).astype(BF16)
        Bt[u] = bb * en
        Kt[u] = kd * en
        lhs_t[u] = jnp.concatenate([bb * ec, kd * ec], axis=0).astype(BF16)
        d1, d2, d3 = split3(eye * gtot[u])
        dd = _dot(jnp.concatenate([d1, d2, d3], axis=0), ones_bd)
        gam[u] = jnp.exp(dd[0:C] + dd[C:2 * C] + dd[2 * C:3 * C])
    A_ab, AA, A_rb = {}, {}, {}
    for u in units:
        sab = _dot_nt(lhs[u], bd(Bt[u]))
        sak = _dot_nt(lhs[u], bd(Kt[u]))
        A_ab[u] = sab[0:C]
        A_rb[u] = jnp.where(incl, sab[C:2 * C], 0.0).astype(BF16)
        AA[u] = jnp.concatenate([jnp.where(before, sak[0:C], 0.0),
                                 jnp.where(incl, sak[C:2 * C], 0.0)], axis=0).astype(BF16)
    X = {u: eye + jnp.where(lvl_masks[0], A_ab[u], 0.0) for u in units}
    for lm in lvl_masks[1:]:
        P = {u: _dot(X[u].astype(BF16), bd(jnp.where(lm, A_ab[u], 0.0))) for u in units}
        X = {u: X[u] + _dot(P[u].astype(BF16), bd(X[u])) for u in units}

    H = [h_sc[gi] for gi in range(groups)]
    for c in order:
        us = [(c, gi) for gi in range(groups)]
        vv = [load(v_ref, u) for u in us]
        AR = [_dot(lhs[u], bd(H[gi])) for gi, u in enumerate(us)]
        AV = [_dot(AA[u], bd(vv[gi])) for gi, u in enumerate(us)]
        U = [_dot(X[u].astype(BF16), bd(AR[gi][0:C] + AV[gi][0:C])) for gi, u in enumerate(us)]
        for gi, u in enumerate(us):
            y_ref[0, c * C:(c + 1) * C, gi * W:(gi + 1) * W] = (
                AR[gi][C:2 * C] + AV[gi][C:2 * C] + _dot(A_rb[u], bd(U[gi])))
        H = [gam[u] * H[gi]
             + fold(_dot_tn(lhs_t[u], jnp.concatenate([U[gi], vv[gi]], axis=0).astype(BF16)))
             for gi, u in enumerate(us)]
    for gi in range(groups):
        h_sc[gi] = H[gi]


def _rw_out_kernel(y0_ref, y1_ref, bonus_ref, g_ref, x_ref, gng_ref, gnb_ref, wo_ref, e_ref, et_ref,
                   lg_ref, lb_ref, o_ref, *, alpha):
    y = y0_ref[0] + y1_ref[0]
    inv_n = 1.0 / RW_HEAD
    mu = _seg_bcast(_seg_sum(y, e_ref) * inv_n, et_ref)
    yc = y - mu
    var = _seg_sum(yc * yc, e_ref) * inv_n
    yn = yc * _seg_bcast(lax.rsqrt(var + RW_GN_EPS), et_ref) * gng_ref[...] + gnb_ref[...]
    yo = (yn + bonus_ref[0]) * g_ref[0]
    m = _dot(yo.astype(BF16), wo_ref[...])
    o_ref[0] = _layer_norm(alpha * x_ref[0] + m, lg_ref[...], lb_ref[...])


def _rwkv_layer(h, mu, w_rkv, w0, w1, w2, a0, a1, a2, g1, g2, k_k, k_a, r_k, gn_g, gn_b, w_o,
                lg, lb, pad, alpha):
    B, Tp, D = h.shape
    heads = D // RW_HEAD

    def pad_cols(w, n):
        return jnp.concatenate([w, jnp.zeros(w.shape[:-1] + (n - w.shape[-1],), w.dtype)], axis=-1)

    def pad_rows(w, n):
        return jnp.concatenate([w, jnp.zeros(w.shape[:-2] + (n - w.shape[-2], w.shape[-1]), w.dtype)], axis=-2)

    g1p = pad_cols(g1, 256).astype(BF16)
    g2p = pad_rows(g2, 256).astype(BF16)
    w1p = jnp.concatenate([pad_cols(w1[0], 128), pad_cols(w1[1], 128)], axis=1).astype(BF16)
    a1p = jnp.concatenate([pad_cols(a1[0], 128), pad_cols(a1[1], 128)], axis=1).astype(BF16)
    w2p = pad_rows(w2, 128).astype(BF16)
    a2p = pad_rows(a2, 128).astype(BF16)
    e = (jnp.arange(D)[:, None] // RW_HEAD == jnp.arange(128)[None, :]).astype(BF16)
    et = e.T

    tm = _row_tile(Tp, 256)
    n = Tp // tm
    prev, nxt = _halo_specs(tm, D, Tp)
    vec = _const_spec((1, D))
    outs = pl.pallas_call(
        functools.partial(_rw_proj_kernel, tm=tm, pad=pad, n_tiles=n),
        grid=(B, n),
        in_specs=[_row_spec(tm, D), prev, nxt, _const_spec((6, D)),
                  _const_spec((D, D)), _const_spec((D, D)), _const_spec((D, D)),
                  _const_spec(g1p.shape), _const_spec(g2p.shape),
                  _const_spec(w1p.shape), _const_spec(w2p.shape), _const_spec((2, D)),
                  _const_spec(a1p.shape), _const_spec(a2p.shape), _const_spec((2, D)),
                  vec, vec, vec, _const_spec(e.shape), _const_spec(et.shape)],
        out_specs=[_row_spec(tm, D)] * 11,
        out_shape=[jax.ShapeDtypeStruct((B, Tp, D), dt) for dt in
                   (BF16, BF16, BF16, F32, F32, F32, F32, BF16, BF16, BF16, BF16)],
        compiler_params=_params(),
        name="rw_proj",
    )(h, h, h, mu, w_rkv[0].astype(BF16), w_rkv[1].astype(BF16), w_rkv[2].astype(BF16), g1p, g2p,
      w1p, w2p, w0, a1p, a2p, a0, k_k.reshape(1, D), k_a.reshape(1, D), r_k.reshape(1, D), e, et)
    r, v, kk, g, bonus, lw0, lw1, b0, b1, kd0, kd1 = outs

    ts = _row_tile(Tp, 256)
    ns = Tp // ts
    ys = []
    for rev, lw, bb, kd in ((False, lw0, b0, kd0), (True, lw1, b1, kd1)):
        if rev:
            spec = pl.BlockSpec((1, ts, D), lambda b, i: (b, ns - 1 - i, 0))
        else:
            spec = _row_spec(ts, D)
        ys.append(pl.pallas_call(
            functools.partial(_rw_scan_kernel, rev=rev, n_sub=ts // RW_CHUNK),
            grid=(B, ns),
            in_specs=[spec] * 6,
            out_specs=spec,
            out_shape=jax.ShapeDtypeStruct((B, Tp, D), F32),
            scratch_shapes=[pltpu.VMEM((D // (RW_GROUP * RW_HEAD), RW_HEAD, RW_GROUP * RW_HEAD), F32)],
            compiler_params=pltpu.CompilerParams(
                dimension_semantics=("parallel", "arbitrary"), vmem_limit_bytes=VMEM_LIMIT),
            name="rw_scan_bwd" if rev else "rw_scan_fwd",
        )(r, lw, kk, bb, kd, v))

    tm = _row_tile(Tp, 640)
    return pl.pallas_call(
        functools.partial(_rw_out_kernel, alpha=alpha),
        grid=(B, Tp // tm),
        in_specs=[_row_spec(tm, D)] * 5 + [vec, vec, _const_spec((D, D)), _const_spec(e.shape),
                                            _const_spec(et.shape), vec, vec],
        out_specs=_row_spec(tm, D),
        out_shape=jax.ShapeDtypeStruct((B, Tp, D), F32),
        compiler_params=_params(),
        name="rw_out",
    )(ys[0], ys[1], bonus, g, h, gn_g.reshape(1, D), gn_b.reshape(1, D), w_o.astype(BF16), e, et,
      lg.reshape(1, D), lb.reshape(1, D))


def _gelu_tanh(z):
    return 0.5 * z * (1.0 + jnp.tanh(math.sqrt(2.0 / math.pi) * (z + 0.044715 * z * z * z)))


def _lru_proj_kernel(x_ref, wg_ref, wu_ref, gate_out, u_out, *, tm, pad):
    i = pl.program_id(1)
    xb = x_ref[0].astype(BF16)
    gate_out[0] = _gelu_tanh(_dot(xb, wg_ref[...]))
    row = i * tm + lax.broadcasted_iota(jnp.int32, (tm, 1), 0)
    u_out[0] = jnp.where(row >= pad, _dot(xb, wu_ref[...]), 0.0)


def _lru_scan_kernel(u_ref, up_ref, un_ref, cw_ref, cb_ref, gw_ref, gb_ref, lam_ref, h_out,
                     a_sc, b_sc, h_sc, *, tm, pad, n_tiles, rev):
    step = pl.program_id(1)
    i = (n_tiles - 1 - step) if rev else step

    @pl.when(step == 0)
    def _():
        h_sc[...] = jnp.zeros_like(h_sc)

    ext = _ext_rows(u_ref[0], up_ref[0], un_ref[0], i, n_tiles, tm, 0)
    uc = cb_ref[...]
    for kt in range(cw_ref.shape[0]):
        uc = uc + cw_ref[kt:kt + 1, :] * _shift(ext, kt - 2, tm)
    width = uc.shape[1]
    nblk = width // LRU_BLOCK
    sp = _softplus(-lam_ref[...])
    row = i * tm + lax.broadcasted_iota(jnp.int32, (tm, 1), 0)
    ok = row >= pad
    for nb in range(nblk):
        cs = slice(nb * LRU_BLOCK, (nb + 1) * LRU_BLOCK)
        ub = uc[:, cs]
        ubb = ub.astype(BF16)
        rg = _sigmoid(_dot(ubb, gw_ref[0, nb]) + gb_ref[0:1, cs])
        ig = _sigmoid(_dot(ubb, gw_ref[1, nb]) + gb_ref[1:2, cs])
        a = jnp.exp(-LRU_C * rg * sp[:, cs])
        a_sc[:, cs] = a
        b_sc[:, cs] = jnp.where(ok, jnp.sqrt(1.0 - a * a) * (ig * ub), 0.0)

    def body(s, hcur):
        t = (tm - 1 - s) if rev else s
        hnew = a_sc[pl.ds(t, 1), :] * hcur + b_sc[pl.ds(t, 1), :]
        h_out[0, pl.ds(t, 1), :] = hnew
        return hnew

    h_sc[...] = lax.fori_loop(0, tm, body, h_sc[...], unroll=8)


def _lru_out_kernel(h0_ref, h1_ref, gate_ref, x_ref, wo_ref, lg_ref, lb_ref, o_ref, *, alpha):
    hh = (h0_ref[0] + h1_ref[0]) * gate_ref[0]
    m = _dot(hh.astype(BF16), wo_ref[...])
    o_ref[0] = _layer_norm(alpha * x_ref[0] + m, lg_ref[...], lb_ref[...])


def _lru_layer(h, w_in, conv_w, conv_b, gate_w, gate_b, lam, w_o, lg, lb, pad, alpha):
    B, Tp, D = h.shape
    width = w_o.shape[0]
    tm = _row_tile(Tp, 640)
    n = Tp // tm
    gate, u = pl.pallas_call(
        functools.partial(_lru_proj_kernel, tm=tm, pad=pad),
        grid=(B, n),
        in_specs=[_row_spec(tm, D), _const_spec((D, width)), _const_spec((D, width))],
        out_specs=[_row_spec(tm, width)] * 2,
        out_shape=[jax.ShapeDtypeStruct((B, Tp, width), F32)] * 2,
        compiler_params=_params(),
        name="lru_proj",
    )(h, w_in[:, :width].astype(BF16), w_in[:, width:].astype(BF16))

    ts = _row_tile(Tp, 256)
    ns = Tp // ts
    nb8 = ts // HALO
    last = Tp // HALO - 1
    hs = []
    for d, rev in ((0, False), (1, True)):
        if rev:
            cur = pl.BlockSpec((1, ts, width), lambda b, s: (b, ns - 1 - s, 0))
            prev = pl.BlockSpec((1, HALO, width), lambda b, s: (b, jnp.maximum((ns - 1 - s) * nb8 - 1, 0), 0))
            nxt = pl.BlockSpec((1, HALO, width), lambda b, s: (b, jnp.minimum((ns - s) * nb8, last), 0))
        else:
            cur = _row_spec(ts, width)
            prev, nxt = _halo_specs(ts, width, Tp)
        hs.append(pl.pallas_call(
            functools.partial(_lru_scan_kernel, tm=ts, pad=pad, n_tiles=ns, rev=rev),
            grid=(B, ns),
            in_specs=[cur, prev, nxt, _const_spec(conv_w.shape), _const_spec((1, width)),
                      _const_spec(gate_w.shape[1:]), _const_spec((2, width)), _const_spec((1, width))],
            out_specs=cur,
            out_shape=jax.ShapeDtypeStruct((B, Tp, width), F32),
            scratch_shapes=[pltpu.VMEM((ts, width), F32), pltpu.VMEM((ts, width), F32),
                            pltpu.VMEM((1, width), F32)],
            compiler_params=pltpu.CompilerParams(
                dimension_semantics=("parallel", "arbitrary"), vmem_limit_bytes=VMEM_LIMIT),
            name="lru_scan_bwd" if rev else "lru_scan_fwd",
        )(u, u, u, conv_w, conv_b.reshape(1, width), gate_w[d].astype(BF16), gate_b[d],
          lam[d].reshape(1, width)))

    return pl.pallas_call(
        functools.partial(_lru_out_kernel, alpha=alpha),
        grid=(B, n),
        in_specs=[_row_spec(tm, width)] * 3 + [_row_spec(tm, D), _const_spec((width, D)),
                                               _const_spec((1, D)), _const_spec((1, D))],
        out_specs=_row_spec(tm, D),
        out_shape=jax.ShapeDtypeStruct((B, Tp, D), F32),
        compiler_params=_params(),
        name="lru_out",
    )(hs[0], hs[1], gate, h, w_o.astype(BF16), lg.reshape(1, D), lb.reshape(1, D))


def kernel(x, positions, meta_tokens, ln_g, ln_b, ffn_w_in, ffn_conv_w, ffn_conv_b, ffn_w_out, mla_w_in, mla_q_norm, mla_w_q_up, mla_kv_norm, mla_w_kv_up, mla_w_o, rw_mu, rw_w_rkv, rw_w0, rw_w1, rw_w2, rw_a0, rw_a1, rw_a2, rw_g1, rw_g2, rw_k_k, rw_k_a, rw_r_k, rw_gn_g, rw_gn_b, rw_w_o, lru_w_in, lru_conv_w, lru_conv_b, lru_gate_w, lru_gate_b, lru_lambda, lru_w_o):
    B, S, D = x.shape
    depth = ln_g.shape[0]
    T = S + N_META
    Tp = -(-T // SEQ_ALIGN) * SEQ_ALIGN
    pad = Tp - T
    alpha = (2.0 * depth) ** 0.25
    dt = x.dtype
    h = jnp.concatenate([jnp.zeros((B, pad, D), dt),
                         jnp.broadcast_to(meta_tokens[None].astype(dt), (B, N_META, D)), x], axis=1)
    pos = jnp.concatenate(
        [jnp.zeros((B, pad), jnp.int32),
         jnp.broadcast_to(jnp.arange(N_META, dtype=jnp.int32)[None, :], (B, N_META)),
         positions + N_META], axis=1)
    inv_freq = ROPE_BASE ** (-jnp.arange(0, MLA_ROPE, 2, dtype=F32) / MLA_ROPE)
    ang = pos.astype(F32)[..., None] * inv_freq
    zeros = jnp.zeros((B, Tp, 128 - MLA_ROPE), F32)
    cos = jnp.cos(ang)
    sin = jnp.sin(ang)
    cos128 = jnp.concatenate([cos, cos, zeros], axis=-1)
    sin128 = jnp.concatenate([sin, sin, zeros], axis=-1)
    for i in range(depth):
        kind = i % 3
        j = i // 3
        lg, lb = ln_g[i, 0], ln_b[i, 0]
        if kind == 0:
            h = _mla_layer(h, cos128, sin128, mla_w_in[j], mla_q_norm[j], mla_w_q_up[j], mla_kv_norm[j],
                           mla_w_kv_up[j], mla_w_o[j], lg, lb, pad, alpha)
        elif kind == 1:
            h = _rwkv_layer(h, rw_mu[j], rw_w_rkv[j], rw_w0[j], rw_w1[j], rw_w2[j], rw_a0[j], rw_a1[j],
                            rw_a2[j], rw_g1[j], rw_g2[j], rw_k_k[j], rw_k_a[j], rw_r_k[j], rw_gn_g[j],
                            rw_gn_b[j], rw_w_o[j], lg, lb, pad, alpha)
        else:
            h = _lru_layer(h, lru_w_in[j], lru_conv_w[j], lru_conv_b[j], lru_gate_w[j], lru_gate_b[j],
                           lru_lambda[j], lru_w_o[j], lg, lb, pad, alpha)
        h = _ffn_layer(h, ffn_w_in[i], ffn_conv_w[i], ffn_conv_b[i], ffn_w_out[i],
                       ln_g[i, 1], ln_b[i, 1], pad, alpha)
    return h[:, pad + N_META:]
```

```python
import functools
import math

import jax
import jax.numpy as jnp
from jax import lax
from jax.experimental import pallas as pl
from jax.experimental.pallas import tpu as pltpu

F32 = jnp.float32
BF16 = jnp.bfloat16

N_META = 16
LN_EPS = 1e-5
RMS_EPS = 1e-6
ROPE_BASE = 10000.0
MLA_NOPE = 128
MLA_ROPE = 64
MLA_V = 128
MLA_Q_RANK = 256
MLA_KV_RANK = 128
RW_HEAD = 64
RW_GN_EPS = 64e-5
LRU_C = 8.0
LRU_BLOCK = 256
SEQ_ALIGN = 256
HALO = 8
RW_CHUNK = 64
RW_GROUP = 4
MASK_NEG = -1e30
MASK_LANE = 128 + MLA_ROPE
ONES_LANE = MASK_LANE + 1
FLASH_UNROLL = 4
FLASH_BIG = 5
VMEM_LIMIT = 56 * 1024 * 1024


def _row_tile(tp, target):
    best = 128
    for t in range(128, min(tp, target) + 1, 128):
        if tp % t == 0:
            best = t
    return best


def _const_spec(shape):
    nd = len(shape)
    return pl.BlockSpec(shape, lambda *_: (0,) * nd, pipeline_mode=pl.Buffered(1))


def _row_spec(tm, c):
    return pl.BlockSpec((1, tm, c), lambda b, i: (b, i, 0))


def _halo_specs(tm, c, tp):
    nb = tm // HALO
    last = tp // HALO - 1
    prev = pl.BlockSpec((1, HALO, c), lambda b, i: (b, jnp.maximum(i * nb - 1, 0), 0))
    nxt = pl.BlockSpec((1, HALO, c), lambda b, i: (b, jnp.minimum((i + 1) * nb, last), 0))
    return prev, nxt


def _params(n_parallel=2):
    return pltpu.CompilerParams(
        dimension_semantics=("parallel",) * n_parallel,
        vmem_limit_bytes=VMEM_LIMIT)


def _dot(a, b):
    return jnp.dot(a, b, preferred_element_type=F32)


def _dot_nt(a, b):
    return lax.dot_general(a, b, (((1,), (1,)), ((), ())), preferred_element_type=F32)


def _dot_tn(a, b):
    return lax.dot_general(a, b, (((0,), (0,)), ((), ())), preferred_element_type=F32)


def _layer_norm(z, g, b):
    mu = jnp.mean(z, axis=-1, keepdims=True)
    zc = z - mu
    var = jnp.mean(zc * zc, axis=-1, keepdims=True)
    return zc * lax.rsqrt(var + LN_EPS) * g + b


def _sigmoid(z):
    return 1.0 / (1.0 + jnp.exp(-z))


def _softplus(z):
    return jnp.maximum(z, 0.0) + jnp.log(1.0 + jnp.exp(-jnp.abs(z)))


def _split2(z):
    hi = z.astype(BF16)
    lo = (z - hi.astype(F32)).astype(BF16)
    return hi, lo


def _ext_rows(x, prev, nxt, i, n_tiles, tm, pad):
    ext = jnp.concatenate([prev, x, nxt], axis=0)
    row = i * tm - HALO + lax.broadcasted_iota(jnp.int32, (tm + 2 * HALO, 1), 0)
    ok = (row >= pad) & (row < n_tiles * tm)
    return jnp.where(ok, ext, 0.0)


def _shift(ext, k, tm):
    n = ext.shape[0]
    if k == 0:
        return ext[HALO:HALO + tm]
    return pltpu.roll(ext, (-k) % n, 0)[HALO:HALO + tm]


def _mla_proj_kernel(x_ref, cos_ref, sin_ref, w_in_ref, qn_ref, kvn_ref, wq_ref, wuk_ref,
                     q_ref, k_ref, *, tm, pad, heads, qscale):
    i = pl.program_id(1)
    x = x_ref[0]
    hp = _dot(x.astype(BF16), w_in_ref[...])
    cq = hp[:, :MLA_Q_RANK]
    ckv = hp[:, MLA_Q_RANK:MLA_Q_RANK + MLA_KV_RANK]
    kpe = hp[:, 384:512]
    kpr = hp[:, 512:640]
    cq = cq * lax.rsqrt(jnp.mean(cq * cq, axis=-1, keepdims=True) + RMS_EPS) * qn_ref[...]
    ckv = ckv * lax.rsqrt(jnp.mean(ckv * ckv, axis=-1, keepdims=True) + RMS_EPS) * kvn_ref[...]
    cos = cos_ref[0]
    sin = sin_ref[0]
    lane = lax.broadcasted_iota(jnp.int32, (tm, 128), 1)
    row = i * tm + lax.broadcasted_iota(jnp.int32, (tm, 128), 0)
    kslab = kpe * cos + kpr * sin
    kslab = jnp.where(lane == MASK_LANE - 128, jnp.where(row < pad, MASK_NEG, 0.0), kslab)
    kslab = jnp.where(lane == ONES_LANE - 128, 1.0, kslab)
    k_ref[0, :, 0:128] = ckv.astype(BF16)
    k_ref[0, :, 128:256] = kslab.astype(BF16)
    q = _dot(cq.astype(BF16), wq_ref[...])
    hw = heads * 128
    for h in range(heads):
        qn = q[:, h * 128:(h + 1) * 128]
        ql = _dot(qn.astype(BF16), wuk_ref[h])
        qp = q[:, hw + h * 128:hw + (h + 1) * 128] * cos + q[:, 2 * hw + h * 128:2 * hw + (h + 1) * 128] * sin
        qp = jnp.where(lane == MASK_LANE - 128, 1.0, qp * qscale)
        q_ref[0, h, :, 0:128] = (ql * qscale).astype(BF16)
        q_ref[0, h, :, 128:256] = qp.astype(BF16)


def _flash_plan(n_tiles):
    big = FLASH_BIG if n_tiles > FLASH_BIG else 1
    n_big, last = divmod(n_tiles - 1, big)
    return big, n_big, last


def _flash_kernel(q_ref, k_ref, o_ref, *scratch, tq, heads, n_tiles, pad):
    all_pad = (pl.program_id(1) + 1) * tq <= pad

    @pl.when(all_pad)
    def _():
        o_ref[...] = jnp.zeros_like(o_ref)

    @pl.when(jnp.logical_not(all_pad))
    def _():
        _flash_block(q_ref, k_ref, o_ref, *scratch, tq=tq, heads=heads, n_tiles=n_tiles)


def _flash_block(q_ref, k_ref, o_ref, m_sc, acc_sc, sa_sc, sb_sc, mca_sc, mcb_sc, *, tq, heads, n_tiles):
    rows = heads * tq
    big, n_big, last = _flash_plan(n_tiles)
    n_chunks = 1 + n_big + (1 if last else 0)
    q = q_ref[0].reshape(rows, 256)
    m_sc[...] = jnp.full((rows, 128), -jnp.inf, F32)
    acc_sc[...] = jnp.zeros((rows, 256), F32)
    slots = ((sa_sc, mca_sc), (sb_sc, mcb_sc))

    def big_start(k):
        return 256 + (k - 1) * (big * 256)

    def chunk(k):
        if k == 0:
            return 0, 1
        if k <= n_big:
            return big_start(k), big
        return big_start(n_big + 1), last

    def scores(start, n_t, slot):
        s_ref, mc_ref = slot
        kc = k_ref[0, pl.ds(start, n_t * 256), :]
        mc = None
        for t in range(n_t):
            st = _dot_nt(q, kc[t * 256:(t + 1) * 256])
            s_ref[t] = st
            mt = jnp.maximum(st[:, 0:128], st[:, 128:256])
            mc = mt if mc is None else jnp.maximum(mc, mt)
        mc_ref[...] = mc

    def softmax_values(start, n_t, slot):
        s_ref, mc_ref = slot
        kc = k_ref[0, pl.ds(start, n_t * 256), :]
        m_old = m_sc[...]
        m_cur = jnp.max(mc_ref[...], axis=-1, keepdims=True)
        m_new = jnp.maximum(m_old, jnp.broadcast_to(m_cur, (rows, 128)))
        alpha = jnp.exp2(m_old - m_new)
        m2 = jnp.tile(m_new, (1, 2))
        p = jnp.concatenate([jnp.exp2((s_ref[t] - m2).astype(BF16)) for t in range(n_t)], axis=1)
        acc_sc[...] = jnp.tile(alpha, (1, 2)) * acc_sc[...] + _dot(p, kc)
        m_sc[...] = m_new

    def static_step(k):
        if k + 1 < n_chunks:
            scores(*chunk(k + 1), slots[(k + 1) % 2])
        softmax_values(*chunk(k), slots[k % 2])

    scores(*chunk(0), slots[0])
    static_step(0)
    n_loop = max(n_big - 1, 0) // FLASH_UNROLL

    def body(i, carry):
        k0 = 1 + FLASH_UNROLL * i
        for r in range(FLASH_UNROLL):
            nxt = pl.multiple_of(big_start(k0 + r + 1), 256)
            cur = pl.multiple_of(big_start(k0 + r), 256)
            scores(nxt, big, slots[r % 2])
            softmax_values(cur, big, slots[(r + 1) % 2])
        return carry

    lax.fori_loop(0, n_loop, body, 0)
    for k in range(1 + FLASH_UNROLL * n_loop, n_chunks):
        static_step(k)
    acc = acc_sc[...]
    out = acc[:, 0:128] / acc[:, ONES_LANE:ONES_LANE + 1]
    for h in range(heads):
        o_ref[0, :, h * 128:(h + 1) * 128] = out[h * tq:(h + 1) * tq].astype(o_ref.dtype)


def _mla_out_kernel(ol_ref, x_ref, wuv_ref, wo_ref, g_ref, b_ref, o_ref, *, heads, alpha):
    ol = ol_ref[0]
    parts = [_dot(ol[:, h * 128:(h + 1) * 128], wuv_ref[h]).astype(BF16) for h in range(heads)]
    o = jnp.concatenate(parts, axis=-1)
    m = _dot(o, wo_ref[...])
    o_ref[0] = _layer_norm(alpha * x_ref[0] + m, g_ref[...], b_ref[...])


def _mla_layer(h, cos128, sin128, w_in, q_norm, w_q_up, kv_norm, w_kv_up, w_o, lg, lb, pad, alpha):
    B, Tp, D = h.shape
    heads = w_kv_up.shape[1]
    hw = heads * 128

    def rot(w):
        return jnp.concatenate([-w[..., MLA_ROPE // 2:], w[..., :MLA_ROPE // 2]], axis=-1)

    def slab(w):
        return jnp.concatenate([w, jnp.zeros(w.shape[:-1] + (128 - MLA_ROPE,), w.dtype)], axis=-1)

    kpe_w = w_in[:, MLA_Q_RANK + MLA_KV_RANK:]
    w_in_aug = jnp.concatenate(
        [w_in[:, :MLA_Q_RANK + MLA_KV_RANK], slab(kpe_w), slab(rot(kpe_w))], axis=1).astype(BF16)
    wq3 = w_q_up.reshape(MLA_Q_RANK, heads, MLA_NOPE + MLA_ROPE)
    wq_pe = wq3[:, :, MLA_NOPE:]
    wq_all = jnp.concatenate(
        [wq3[:, :, :MLA_NOPE].reshape(MLA_Q_RANK, hw),
         slab(wq_pe).reshape(MLA_Q_RANK, hw),
         slab(rot(wq_pe)).reshape(MLA_Q_RANK, hw)], axis=1).astype(BF16)
    w_ukT = jnp.transpose(w_kv_up[:, :, :MLA_NOPE], (1, 2, 0)).astype(BF16)
    w_uv = jnp.transpose(w_kv_up[:, :, MLA_NOPE:], (1, 0, 2)).astype(BF16)
    qscale = (MLA_NOPE + MLA_ROPE) ** -0.5 * math.log2(math.e)

    tm = _row_tile(Tp, 256)
    n = Tp // tm
    q, k = pl.pallas_call(
        functools.partial(_mla_proj_kernel, tm=tm, pad=pad, heads=heads, qscale=qscale),
        grid=(B, n),
        in_specs=[_row_spec(tm, D), _row_spec(tm, 128), _row_spec(tm, 128),
                  _const_spec(w_in_aug.shape), _const_spec((1, MLA_Q_RANK)), _const_spec((1, MLA_KV_RANK)),
                  _const_spec(wq_all.shape), _const_spec(w_ukT.shape)],
        out_specs=[pl.BlockSpec((1, heads, tm, 256), lambda b, i: (b, 0, i, 0)), _row_spec(tm, 256)],
        out_shape=[jax.ShapeDtypeStruct((B, heads, Tp, 256), BF16),
                   jax.ShapeDtypeStruct((B, Tp, 256), BF16)],
        compiler_params=_params(),
        name="mla_proj",
    )(h, cos128, sin128, w_in_aug, q_norm.reshape(1, -1), kv_norm.reshape(1, -1), wq_all, w_ukT)

    tq = 64
    n_tiles = Tp // 256
    big = _flash_plan(n_tiles)[0]
    rows = heads * tq
    o_lat = pl.pallas_call(
        functools.partial(_flash_kernel, tq=tq, heads=heads, n_tiles=n_tiles, pad=pad),
        grid=(B, Tp // tq),
        in_specs=[pl.BlockSpec((1, heads, tq, 256), lambda b, i: (b, 0, i, 0)),
                  pl.BlockSpec((1, Tp, 256), lambda b, i: (b, 0, 0))],
        out_specs=_row_spec(tq, hw),
        out_shape=jax.ShapeDtypeStruct((B, Tp, hw), BF16),
        scratch_shapes=[pltpu.VMEM((rows, 128), F32), pltpu.VMEM((rows, 256), F32),
                        pltpu.VMEM((big, rows, 256), F32), pltpu.VMEM((big, rows, 256), F32),
                        pltpu.VMEM((rows, 128), F32), pltpu.VMEM((rows, 128), F32)],
        compiler_params=_params(),
        name="mla_flash",
    )(q, k)

    tm = _row_tile(Tp, 640)
    return pl.pallas_call(
        functools.partial(_mla_out_kernel, heads=heads, alpha=alpha),
        grid=(B, Tp // tm),
        in_specs=[_row_spec(tm, hw), _row_spec(tm, D), _const_spec(w_uv.shape),
                  _const_spec(w_o.shape), _const_spec((1, D)), _const_spec((1, D))],
        out_specs=_row_spec(tm, D),
        out_shape=jax.ShapeDtypeStruct((B, Tp, D), F32),
        compiler_params=_params(),
        name="mla_out",
    )(o_lat, h, w_uv, w_o.astype(BF16), lg.reshape(1, D), lb.reshape(1, D))


def _ffn_kernel(x_ref, xp_ref, xn_ref, wg_ref, wu_ref, cw_ref, cb_ref, wo_ref, g_ref, b_ref, o_ref,
                *, tm, pad, n_tiles, ff_chunk, alpha):
    i = pl.program_id(1)
    x = x_ref[0]
    xb = x.astype(BF16)
    ext = jnp.concatenate([xp_ref[0], x, xn_ref[0]], axis=0).astype(BF16)
    n_ext = tm + 2 * HALO
    row = i * tm - HALO + lax.broadcasted_iota(jnp.int32, (n_ext, 1), 0)
    ok = (row >= pad) & (row < n_tiles * tm)
    d_ff = wg_ref.shape[1]
    acc = jnp.zeros((tm, x.shape[1]), F32)
    for j in range(d_ff // ff_chunk):
        cs = slice(j * ff_chunk, (j + 1) * ff_chunk)
        g = jnp.where(ok, _dot(ext, wg_ref[:, cs]), 0.0)
        u = _dot(xb, wu_ref[:, cs])
        gc = (cw_ref[0:1, cs] * _shift(g, -1, tm) + cw_ref[1:2, cs] * _shift(g, 0, tm)
              + cw_ref[2:3, cs] * _shift(g, 1, tm) + cb_ref[:, cs])
        act = gc * _sigmoid(gc) * u
        acc = acc + _dot(act.astype(BF16), wo_ref[cs, :])
    o_ref[0] = _layer_norm(alpha * x + acc, g_ref[...], b_ref[...])


def _ffn_layer(h, w_in, conv_w, conv_b, w_out, lg, lb, pad, alpha):
    B, Tp, D = h.shape
    d_ff = w_out.shape[0]
    tm = _row_tile(Tp, 640)
    n = Tp // tm
    prev, nxt = _halo_specs(tm, D, Tp)
    wg = w_in[:, :d_ff].astype(BF16)
    wu = w_in[:, d_ff:].astype(BF16)
    return pl.pallas_call(
        functools.partial(_ffn_kernel, tm=tm, pad=pad, n_tiles=n, ff_chunk=d_ff, alpha=alpha),
        grid=(B, n),
        in_specs=[_row_spec(tm, D), prev, nxt, _const_spec(wg.shape), _const_spec(wu.shape),
                  _const_spec(conv_w.shape), _const_spec((1, d_ff)), _const_spec(w_out.shape),
                  _const_spec((1, D)), _const_spec((1, D))],
        out_specs=_row_spec(tm, D),
        out_shape=jax.ShapeDtypeStruct((B, Tp, D), F32),
        compiler_params=_params(),
        name="ffn",
    )(h, h, h, wg, wu, conv_w, conv_b.reshape(1, d_ff), w_out.astype(BF16),
      lg.reshape(1, D), lb.reshape(1, D))


def _seg_sum(z, e_ref):
    hi, lo = _split2(z)
    return _dot(hi, e_ref[...]) + _dot(lo, e_ref[...])


def _seg_bcast(zs, et_ref):
    hi, lo = _split2(zs)
    return _dot(hi, et_ref[...]) + _dot(lo, et_ref[...])


def _rw_proj_kernel(x_ref, xp_ref, xn_ref, mu_ref, wr_ref, wk_ref, wv_ref, g1_ref, g2_ref,
                    w1_ref, w2_ref, w0_ref, a1_ref, a2_ref, a0_ref, kk_ref, ka_ref, rk_ref,
                    e_ref, et_ref,
                    r_out, v_out, kk_out, g_out, bonus_out, lw0_out, lw1_out, b0_out, b1_out,
                    kd0_out, kd1_out, *, tm, pad, n_tiles):
    i = pl.program_id(1)
    ext = _ext_rows(x_ref[0], xp_ref[0], xn_ref[0], i, n_tiles, tm, pad)
    x = _shift(ext, 0, tm)
    xx = 0.5 * (_shift(ext, -1, tm) + _shift(ext, 1, tm)) - x
    mix = [(x + xx * mu_ref[c:c + 1, :]).astype(BF16) for c in range(6)]
    xr, xw, xk, xv, xa, xg = mix
    row = i * tm + lax.broadcasted_iota(jnp.int32, (tm, 1), 0)
    ok = row >= pad
    r = _dot(xr, wr_ref[...])
    k = jnp.where(ok, _dot(xk, wk_ref[...]), 0.0)
    v = jnp.where(ok, _dot(xv, wv_ref[...]), 0.0)
    g = _dot(_sigmoid(_dot(xg, g1_ref[...])).astype(BF16), g2_ref[...])
    tw = jnp.tanh(_dot(xw, w1_ref[...])).astype(BF16)
    ta = _dot(xa, a1_ref[...]).astype(BF16)
    kk = k * kk_ref[...]
    ss = _seg_sum(kk * kk, e_ref)
    kk = kk * _seg_bcast(lax.rsqrt(jnp.maximum(ss, 1e-24)), et_ref)
    r_out[0] = r.astype(r_out.dtype)
    v_out[0] = v.astype(v_out.dtype)
    kk_out[0] = kk.astype(kk_out.dtype)
    g_out[0] = g
    kd_sum = None
    for d, (lw_out, b_out, kd_out) in enumerate(((lw0_out, b0_out, kd0_out), (lw1_out, b1_out, kd1_out))):
        wl = w0_ref[d:d + 1, :] + _dot(tw[:, d * 128:(d + 1) * 128], w2_ref[d])
        w_log = -_softplus(-wl) - 0.5
        lw_out[0] = -jnp.exp(w_log)
        a = _sigmoid(a0_ref[d:d + 1, :] + _dot(ta[:, d * 128:(d + 1) * 128], a2_ref[d]))
        kd = k * (1.0 + (a - 1.0) * ka_ref[...])
        kd_out[0] = kd.astype(kd_out.dtype)
        b_out[0] = (kk * a).astype(b_out.dtype)
        kd_sum = kd if kd_sum is None else kd_sum + kd
    bs = _seg_sum(r * kd_sum * rk_ref[...], e_ref)
    bonus_out[0] = _seg_bcast(bs, et_ref) * v


def _rw_scan_kernel(r_ref, lw_ref, kk_ref, b_ref, kd_ref, v_ref, y_ref, h_sc, *, rev, n_sub):
    C = RW_CHUNK
    W = RW_GROUP * RW_HEAD
    groups = r_ref.shape[2] // W

    @pl.when(pl.program_id(1) == 0)
    def _():
        h_sc[...] = jnp.zeros_like(h_sc)

    ti = lax.broadcasted_iota(jnp.int32, (C, W), 0)
    si = lax.broadcasted_iota(jnp.int32, (C, W), 1) % RW_HEAD
    if rev:
        before = si > ti
    else:
        before = si < ti
    incl = before | (si == ti)
    eye = (si == ti).astype(F32)
    lvl_masks = []
    m = 1
    while m < C:
        same = (ti // (2 * m)) == (si // (2 * m))
        t_hi = (ti // m) % 2 == 1
        s_hi = (si // m) % 2 == 1
        if rev:
            lvl_masks.append(same & (~t_hi) & s_hi)
        else:
            lvl_masks.append(same & t_hi & (~s_hi))
        m *= 2
    bi = lax.broadcasted_iota(jnp.int32, (W, W), 0) // RW_HEAD
    bj = lax.broadcasted_iota(jnp.int32, (W, W), 1) // RW_HEAD
    bd_mask = bi == bj
    ones_bd = bd_mask.astype(BF16)
    ci =lax.broadcasted_iota(jnp.int32, (C, C), 0)
    cj = lax.broadcasted_iota(jnp.int32, (C, C), 1)
    tri = ((cj >= ci) if rev else (cj <= ci)).astype(BF16)

    def bd(z):
        zb = z.astype(BF16)
        return jnp.concatenate([zb] * RW_GROUP, axis=0) * ones_bd

    def fold(z):
        zm = jnp.where(bd_mask, z, 0.0)
        out = zm[0:RW_HEAD]
        for a in range(1, RW_GROUP):
            out = out + zm[a * RW_HEAD:(a + 1) * RW_HEAD]
        return out

    order = list(range(n_sub - 1, -1, -1)) if rev else list(range(n_sub))
    units = [(c, gi) for c in order for gi in range(groups)]

    def load(ref, u):
        c, gi = u
        return ref[0, c * C:(c + 1) * C, gi * W:(gi + 1) * W].astype(F32)

    def split3(z):
        z1 = z.astype(BF16)
        rem = z - z1.astype(F32)
        z2 = rem.astype(BF16)
        z3 = (rem - z2.astype(F32)).astype(BF16)
        return z1, z2, z3

    gsum, gtot = {}, {}
    for u in units:
        l1, l2, l3 = split3(load(lw_ref, u))
        gs = _dot(tri, l1) + _dot(tri, l2) + _dot(tri, l3)
        gsum[u] = gs
        gtot[u] = gs[0:1] if rev else gs[C - 1:C]
    lhs, Bt, Kt, lhs_t, gam = {}, {}, {}, {}, {}
    for u in units:
        gs = gsum[u]
        bb = load(b_ref, u)
        kd = load(kd_ref, u)
        en = jnp.exp(-gs)
        ec = jnp.exp(gtot[u] - gs)
        At = -load(kk_ref, u) * jnp.exp(gs - load(lw_ref, u))
        Rt = load(r_ref, u) * jnp.exp(gs)
        lhs[u] = jnp.concatenate([At, Rt], axis=0).astype(BF16)
        Bt[u] = bb * en
        Kt[u] = kd * en
        lhs_t[u] = jnp.concatenate([bb * ec, kd * ec], axis=0).astype(BF16)
        d1, d2, d3 = split3(eye * gtot[u])
        dd = _dot(jnp.concatenate([d1, d2, d3], axis=0), ones_bd)
        gam[u] = jnp.exp(dd[0:C] + dd[C:2 * C] + dd[2 * C:3 * C])
    A_ab, AA, A_rb = {}, {}, {}
    for u in units:
        sab = _dot_nt(lhs[u], bd(Bt[u]))
        sak = _dot_nt(lhs[u], bd(Kt[u]))
        A_ab[u] = sab[0:C]
        A_rb[u] = jnp.where(incl, sab[C:2 * C], 0.0).astype(BF16)
        AA[u] = jnp.concatenate([jnp.where(before, sak[0:C], 0.0),
                                 jnp.where(incl, sak[C:2 * C], 0.0)], axis=0).astype(BF16)
    X = {u: eye + jnp.where(lvl_masks[0], A_ab[u], 0.0) for u in units}
    for lm in lvl_masks[1:]:
        P = {u: _dot(X[u].astype(BF16), bd(jnp.where(lm, A_ab[u], 0.0))) for u in units}
        X = {u: X[u] + _dot(P[u].astype(BF16), bd(X[u])) for u in units}

    H = [h_sc[gi] for gi in range(groups)]
    for c in order:
        us = [(c, gi) for gi in range(groups)]
        vv = [load(v_ref, u) for u in us]
        AR = [_dot(lhs[u], bd(H[gi])) for gi, u in enumerate(us)]
        AV = [_dot(AA[u], bd(vv[gi])) for gi, u in enumerate(us)]
        U = [_dot(X[u].astype(BF16), bd(AR[gi][0:C] + AV[gi][0:C])) for gi, u in enumerate(us)]
        for gi, u in enumerate(us):
            y_ref[0, c * C:(c + 1) * C, gi * W:(gi + 1) * W] = (
                AR[gi][C:2 * C] + AV[gi][C:2 * C] + _dot(A_rb[u], bd(U[gi])))
        H = [gam[u] * H[gi]
             + fold(_dot_tn(lhs_t[u], jnp.concatenate([U[gi], vv[gi]], axis=0).astype(BF16)))
             for gi, u in enumerate(us)]
    for gi in range(groups):
        h_sc[gi] = H[gi]


def _rw_out_kernel(y0_ref, y1_ref, bonus_ref, g_ref, x_ref, gng_ref, gnb_ref, wo_ref, e_ref, et_ref,
                   lg_ref, lb_ref, o_ref, *, alpha):
    y = y0_ref[0] + y1_ref[0]
    inv_n = 1.0 / RW_HEAD
    mu = _seg_bcast(_seg_sum(y, e_ref) * inv_n, et_ref)
    yc = y - mu
    var = _seg_sum(yc * yc, e_ref) * inv_n
    yn = yc * _seg_bcast(lax.rsqrt(var + RW_GN_EPS), et_ref) * gng_ref[...] + gnb_ref[...]
    yo = (yn + bonus_ref[0]) * g_ref[0]
    m = _dot(yo.astype(BF16), wo_ref[...])
    o_ref[0] = _layer_norm(alpha * x_ref[0] + m, lg_ref[...], lb_ref[...])


def _rwkv_layer(h, mu, w_rkv, w0, w1, w2, a0, a1, a2, g1, g2, k_k, k_a, r_k, gn_g, gn_b, w_o,
                lg, lb, pad, alpha):
    B, Tp, D = h.shape
    heads = D // RW_HEAD

    def pad_cols(w, n):
        return jnp.concatenate([w, jnp.zeros(w.shape[:-1] + (n - w.shape[-1],), w.dtype)], axis=-1)

    def pad_rows(w, n):
        return jnp.concatenate([w, jnp.zeros(w.shape[:-2] + (n - w.shape[-2], w.shape[-1]), w.dtype)], axis=-2)

    g1p = pad_cols(g1, 256).astype(BF16)
    g2p = pad_rows(g2, 256).astype(BF16)
    w1p = jnp.concatenate([pad_cols(w1[0], 128), pad_cols(w1[1], 128)], axis=1).astype(BF16)
    a1p = jnp.concatenate([pad_cols(a1[0], 128), pad_cols(a1[1], 128)], axis=1).astype(BF16)
    w2p = pad_rows(w2, 128).astype(BF16)
    a2p = pad_rows(a2, 128).astype(BF16)
    e = (jnp.arange(D)[:, None] // RW_HEAD == jnp.arange(128)[None, :]).astype(BF16)
    et = e.T

    tm = _row_tile(Tp, 256)
    n = Tp // tm
    prev, nxt = _halo_specs(tm, D, Tp)
    vec = _const_spec((1, D))
    outs = pl.pallas_call(
        functools.partial(_rw_proj_kernel, tm=tm, pad=pad, n_tiles=n),
        grid=(B, n),
        in_specs=[_row_spec(tm, D), prev, nxt, _const_spec((6, D)),
                  _const_spec((D, D)), _const_spec((D, D)), _const_spec((D, D)),
                  _const_spec(g1p.shape), _const_spec(g2p.shape),
                  _const_spec(w1p.shape), _const_spec(w2p.shape), _const_spec((2, D)),
                  _const_spec(a1p.shape), _const_spec(a2p.shape), _const_spec((2, D)),
                  vec, vec, vec, _const_spec(e.shape), _const_spec(et.shape)],
        out_specs=[_row_spec(tm, D)] * 11,
        out_shape=[jax.ShapeDtypeStruct((B, Tp, D), dt) for dt in
                   (BF16, BF16, BF16, F32, F32, F32, F32, BF16, BF16, BF16, BF16)],
        compiler_params=_params(),
        name="rw_proj",
    )(h, h, h, mu, w_rkv[0].astype(BF16), w_rkv[1].astype(BF16), w_rkv[2].astype(BF16), g1p, g2p,
      w1p, w2p, w0, a1p, a2p, a0, k_k.reshape(1, D), k_a.reshape(1, D), r_k.reshape(1, D), e, et)
    r, v, kk, g, bonus, lw0, lw1, b0, b1, kd0, kd1 = outs

    ts = _row_tile(Tp, 256)
    ns = Tp // ts
    ys = []
    for rev, lw, bb, kd in ((False, lw0, b0, kd0), (True, lw1, b1, kd1)):
        if rev:
            spec = pl.BlockSpec((1, ts, D), lambda b, i: (b, ns - 1 - i, 0))
        else:
            spec = _row_spec(ts, D)
        ys.append(pl.pallas_call(
            functools.partial(_rw_scan_kernel, rev=rev, n_sub=ts // RW_CHUNK),
            grid=(B, ns),
            in_specs=[spec] * 6,
            out_specs=spec,
            out_shape=jax.ShapeDtypeStruct((B, Tp, D), F32),
            scratch_shapes=[pltpu.VMEM((D // (RW_GROUP * RW_HEAD), RW_HEAD, RW_GROUP * RW_HEAD), F32)],
            compiler_params=pltpu.CompilerParams(
                dimension_semantics=("parallel", "arbitrary"), vmem_limit_bytes=VMEM_LIMIT),
            name="rw_scan_bwd" if rev else "rw_scan_fwd",
        )(r, lw, kk, bb, kd, v))

    tm = _row_tile(Tp, 640)
    return pl.pallas_call(
        functools.partial(_rw_out_kernel, alpha=alpha),
        grid=(B, Tp // tm),
        in_specs=[_row_spec(tm, D)] * 5 + [vec, vec, _const_spec((D, D)), _const_spec(e.shape),
                                            _const_spec(et.shape), vec, vec],
        out_specs=_row_spec(tm, D),
        out_shape=jax.ShapeDtypeStruct((B, Tp, D), F32),
        compiler_params=_params(),
        name="rw_out",
    )(ys[0], ys[1], bonus, g, h, gn_g.reshape(1, D), gn_b.reshape(1, D), w_o.astype(BF16), e, et,
      lg.reshape(1, D), lb.reshape(1, D))


def _gelu_tanh(z):
    return 0.5 * z * (1.0 + jnp.tanh(math.sqrt(2.0 / math.pi) * (z + 0.044715 * z * z * z)))


def _lru_proj_kernel(x_ref, wg_ref, wu_ref, gate_out, u_out, *, tm, pad):
    i = pl.program_id(1)
    xb = x_ref[0].astype(BF16)
    gate_out[0] = _gelu_tanh(_dot(xb, wg_ref[...]))
    row = i * tm + lax.broadcasted_iota(jnp.int32, (tm, 1), 0)
    u_out[0] = jnp.where(row >= pad, _dot(xb, wu_ref[...]), 0.0)


def _lru_scan_kernel(u_ref, up_ref, un_ref, cw_ref, cb_ref, gw_ref, gb_ref, lam_ref, h_out,
                     a_sc, b_sc, h_sc, *, tm, pad, n_tiles, rev):
    step = pl.program_id(1)
    i = (n_tiles - 1 - step) if rev else step

    @pl.when(step == 0)
    def _():
        h_sc[...] = jnp.zeros_like(h_sc)

    ext = _ext_rows(u_ref[0], up_ref[0], un_ref[0], i, n_tiles, tm, 0)
    uc = cb_ref[...]
    for kt in range(cw_ref.shape[0]):
        uc = uc + cw_ref[kt:kt + 1, :] * _shift(ext, kt - 2, tm)
    width = uc.shape[1]
    nblk = width // LRU_BLOCK
    sp = _softplus(-lam_ref[...])
    row = i * tm + lax.broadcasted_iota(jnp.int32, (tm, 1), 0)
    ok = row >= pad
    for nb in range(nblk):
        cs = slice(nb * LRU_BLOCK, (nb + 1) * LRU_BLOCK)
        ub = uc[:, cs]
        ubb = ub.astype(BF16)
        rg = _sigmoid(_dot(ubb, gw_ref[0, nb]) + gb_ref[0:1, cs])
        ig = _sigmoid(_dot(ubb, gw_ref[1, nb]) + gb_ref[1:2, cs])
        a = jnp.exp(-LRU_C * rg * sp[:, cs])
        a_sc[:, cs] = a
        b_sc[:, cs] = jnp.where(ok, jnp.sqrt(1.0 - a * a) * (ig * ub), 0.0)

    def body(s, hcur):
        t = (tm - 1 - s) if rev else s
        hnew = a_sc[pl.ds(t, 1), :] * hcur + b_sc[pl.ds(t, 1), :]
        h_out[0, pl.ds(t, 1), :] = hnew
        return hnew

    h_sc[...] = lax.fori_loop(0, tm, body, h_sc[...], unroll=8)


def _lru_out_kernel(h0_ref, h1_ref, gate_ref, x_ref, wo_ref, lg_ref, lb_ref, o_ref, *, alpha):
    hh = (h0_ref[0] + h1_ref[0]) * gate_ref[0]
    m = _dot(hh.astype(BF16), wo_ref[...])
    o_ref[0] = _layer_norm(alpha * x_ref[0] + m, lg_ref[...], lb_ref[...])


def _lru_layer(h, w_in, conv_w, conv_b, gate_w, gate_b, lam, w_o, lg, lb, pad, alpha):
    B, Tp, D = h.shape
    width = w_o.shape[0]
    tm = _row_tile(Tp, 640)
    n = Tp // tm
    gate, u = pl.pallas_call(
        functools.partial(_lru_proj_kernel, tm=tm, pad=pad),
        grid=(B, n),
        in_specs=[_row_spec(tm, D), _const_spec((D, width)), _const_spec((D, width))],
        out_specs=[_row_spec(tm, width)] * 2,
        out_shape=[jax.ShapeDtypeStruct((B, Tp, width), F32)] * 2,
        compiler_params=_params(),
        name="lru_proj",
    )(h, w_in[:, :width].astype(BF16), w_in[:, width:].astype(BF16))

    ts = _row_tile(Tp, 256)
    ns = Tp // ts
    nb8 = ts // HALO
    last = Tp // HALO - 1
    hs = []
    for d, rev in ((0, False), (1, True)):
        if rev:
            cur = pl.BlockSpec((1, ts, width), lambda b, s: (b, ns - 1 - s, 0))
            prev = pl.BlockSpec((1, HALO, width), lambda b, s: (b, jnp.maximum((ns - 1 - s) * nb8 - 1, 0), 0))
            nxt = pl.BlockSpec((1, HALO, width), lambda b, s: (b, jnp.minimum((ns - s) * nb8, last), 0))
        else:
            cur = _row_spec(ts, width)
            prev, nxt = _halo_specs(ts, width, Tp)
        hs.append(pl.pallas_call(
            functools.partial(_lru_scan_kernel, tm=ts, pad=pad, n_tiles=ns, rev=rev),
            grid=(B, ns),
            in_specs=[cur, prev, nxt, _const_spec(conv_w.shape), _const_spec((1, width)),
                      _const_spec(gate_w.shape[1:]), _const_spec((2, width)), _const_spec((1, width))],
            out_specs=cur,
            out_shape=jax.ShapeDtypeStruct((B, Tp, width), F32),
            scratch_shapes=[pltpu.VMEM((ts, width), F32), pltpu.VMEM((ts, width), F32),
                            pltpu.VMEM((1, width), F32)],
            compiler_params=pltpu.CompilerParams(
                dimension_semantics=("parallel", "arbitrary"), vmem_limit_bytes=VMEM_LIMIT),
            name="lru_scan_bwd" if rev else "lru_scan_fwd",
        )(u, u, u, conv_w, conv_b.reshape(1, width), gate_w[d].astype(BF16), gate_b[d],
          lam[d].reshape(1, width)))

    return pl.pallas_call(
        functools.partial(_lru_out_kernel, alpha=alpha),
        grid=(B, n),
        in_specs=[_row_spec(tm, width)] * 3 + [_row_spec(tm, D), _const_spec((width, D)),
                                               _const_spec((1, D)), _const_spec((1, D))],
        out_specs=_row_spec(tm, D),
        out_shape=jax.ShapeDtypeStruct((B, Tp, D), F32),
        compiler_params=_params(),
        name="lru_out",
    )(hs[0], hs[1], gate, h, w_o.astype(BF16), lg.reshape(1, D), lb.reshape(1, D))


def kernel(x, positions, meta_tokens, ln_g, ln_b, ffn_w_in, ffn_conv_w, ffn_conv_b, ffn_w_out, mla_w_in, mla_q_norm, mla_w_q_up, mla_kv_norm, mla_w_kv_up, mla_w_o, rw_mu, rw_w_rkv, rw_w0, rw_w1, rw_w2, rw_a0, rw_a1, rw_a2, rw_g1, rw_g2, rw_k_k, rw_k_a, rw_r_k, rw_gn_g, rw_gn_b, rw_w_o, lru_w_in, lru_conv_w, lru_conv_b, lru_gate_w, lru_gate_b, lru_lambda, lru_w_o):
    B, S, D = x.shape
    depth = ln_g.shape[0]
    T = S + N_META
    Tp = -(-T // SEQ_ALIGN) * SEQ_ALIGN
    pad = Tp - T
    alpha = (2.0 * depth) ** 0.25
    dt = x.dtype
    h = jnp.concatenate([jnp.zeros((B, pad, D), dt),
                         jnp.broadcast_to(meta_tokens[None].astype(dt), (B, N_META, D)), x], axis=1)
    pos = jnp.concatenate(
        [jnp.zeros((B, pad), jnp.int32),
         jnp.broadcast_to(jnp.arange(N_META, dtype=jnp.int32)[None, :], (B, N_META)),
         positions + N_META], axis=1)
    inv_freq = ROPE_BASE ** (-jnp.arange(0, MLA_ROPE, 2, dtype=F32) / MLA_ROPE)
    ang = pos.astype(F32)[..., None] * inv_freq
    zeros = jnp.zeros((B, Tp, 128 - MLA_ROPE), F32)
    cos = jnp.cos(ang)
    sin = jnp.sin(ang)
    cos128 = jnp.concatenate([cos, cos, zeros], axis=-1)
    sin128 = jnp.concatenate([sin, sin, zeros], axis=-1)
    for i in range(depth):
        kind = i % 3
        j = i // 3
        lg, lb = ln_g[i, 0], ln_b[i, 0]
        if kind == 0:
            h = _mla_layer(h, cos128, sin128, mla_w_in[j], mla_q_norm[j], mla_w_q_up[j], mla_kv_norm[j],
                           mla_w_kv_up[j], mla_w_o[j], lg, lb, pad, alpha)
        elif kind == 1:
            h = _rwkv_layer(h, rw_mu[j], rw_w_rkv[j], rw_w0[j], rw_w1[j], rw_w2[j], rw_a0[j], rw_a1[j],
                            rw_a2[j], rw_g1[j], rw_g2[j], rw_k_k[j], rw_k_a[j], rw_r_k[j], rw_gn_g[j],
                            rw_gn_b[j], rw_w_o[j], lg, lb, pad, alpha)
        else:
            h = _lru_layer(h, lru_w_in[j], lru_conv_w[j], lru_conv_b[j], lru_gate_w[j], lru_gate_b[j],
                           lru_lambda[j], lru_w_o[j], lg, lb, pad, alpha)
        h = _ffn_layer(h, ffn_w_in[i], ffn_conv_w[i], ffn_conv_b[i], ffn_w_out[i],
                       ln_g[i, 1], ln_b[i, 1], pad, alpha)
    return h[:, pad + N_META:]
```

```python
import functools
import math

import jax
import jax.numpy as jnp
from jax import lax
from jax.experimental import pallas as pl
from jax.experimental.pallas import tpu as pltpu

F32 = jnp.float32
BF16 = jnp.bfloat16

N_META = 16
LN_EPS = 1e-5
RMS_EPS = 1e-6
ROPE_BASE = 10000.0
MLA_NOPE = 128
MLA_ROPE = 64
MLA_V = 128
MLA_Q_RANK = 256
MLA_KV_RANK = 128
RW_HEAD = 64
RW_GN_EPS = 64e-5
LRU_C = 8.0
LRU_BLOCK = 256
SEQ_ALIGN = 256
HALO = 8
RW_CHUNK = 64
RW_GROUP = 4
MASK_NEG = -1e30
MASK_LANE = 128 + MLA_ROPE
ONES_LANE = MASK_LANE + 1
FLASH_UNROLL = 4
FLASH_BIG = 8
VMEM_LIMIT = 56 * 1024 * 1024


def _row_tile(tp, target):
    best = 128
    for t in range(128, min(tp, target) + 1, 128):
        if tp % t == 0:
            best = t
    return best


def _const_spec(shape):
    nd = len(shape)
    return pl.BlockSpec(shape, lambda *_: (0,) * nd, pipeline_mode=pl.Buffered(1))


def _row_spec(tm, c):
    return pl.BlockSpec((1, tm, c), lambda b, i: (b, i, 0))


def _halo_specs(tm, c, tp):
    nb = tm // HALO
    last = tp // HALO - 1
    prev = pl.BlockSpec((1, HALO, c), lambda b, i: (b, jnp.maximum(i * nb - 1, 0), 0))
    nxt = pl.BlockSpec((1, HALO, c), lambda b, i: (b, jnp.minimum((i + 1) * nb, last), 0))
    return prev, nxt


def _params(n_parallel=2):
    return pltpu.CompilerParams(
        dimension_semantics=("parallel",) * n_parallel,
        vmem_limit_bytes=VMEM_LIMIT)


def _dot(a, b):
    return jnp.dot(a, b, preferred_element_type=F32)


def _dot_nt(a, b):
    return lax.dot_general(a, b, (((1,), (1,)), ((), ())), preferred_element_type=F32)


def _dot_tn(a, b):
    return lax.dot_general(a, b, (((0,), (0,)), ((), ())), preferred_element_type=F32)


def _layer_norm(z, g, b):
    mu = jnp.mean(z, axis=-1, keepdims=True)
    zc = z - mu
    var = jnp.mean(zc * zc, axis=-1, keepdims=True)
    return zc * lax.rsqrt(var + LN_EPS) * g + b


def _sigmoid(z):
    return 1.0 / (1.0 + jnp.exp(-z))


def _softplus(z):
    return jnp.maximum(z, 0.0) + jnp.log(1.0 + jnp.exp(-jnp.abs(z)))


def _split2(z):
    hi = z.astype(BF16)
    lo = (z - hi.astype(F32)).astype(BF16)
    return hi, lo


def _ext_rows(x, prev, nxt, i, n_tiles, tm, pad):
    ext = jnp.concatenate([prev, x, nxt], axis=0)
    row = i * tm - HALO + lax.broadcasted_iota(jnp.int32, (tm + 2 * HALO, 1), 0)
    ok = (row >= pad) & (row < n_tiles * tm)
    return jnp.where(ok, ext, 0.0)


def _shift(ext, k, tm):
    n = ext.shape[0]
    if k == 0:
        return ext[HALO:HALO + tm]
    return pltpu.roll(ext, (-k) % n, 0)[HALO:HALO + tm]


def _mla_proj_kernel(x_ref, cos_ref, sin_ref, w_in_ref, qn_ref, kvn_ref, wq_ref, wuk_ref,
                     q_ref, k_ref, *, tm, pad, heads, qscale):
    i = pl.program_id(1)
    x = x_ref[0]
    hp = _dot(x.astype(BF16), w_in_ref[...])
    cq = hp[:, :MLA_Q_RANK]
    ckv = hp[:, MLA_Q_RANK:MLA_Q_RANK + MLA_KV_RANK]
    kpe = hp[:, 384:512]
    kpr = hp[:, 512:640]
    cq = cq * lax.rsqrt(jnp.mean(cq * cq, axis=-1, keepdims=True) + RMS_EPS) * qn_ref[...]
    ckv = ckv * lax.rsqrt(jnp.mean(ckv * ckv, axis=-1, keepdims=True) + RMS_EPS) * kvn_ref[...]
    cos = cos_ref[0]
    sin = sin_ref[0]
    lane = lax.broadcasted_iota(jnp.int32, (tm, 128), 1)
    row = i * tm + lax.broadcasted_iota(jnp.int32, (tm, 128), 0)
    kslab = kpe * cos + kpr * sin
    kslab = jnp.where(lane == MASK_LANE - 128, jnp.where(row < pad, MASK_NEG, 0.0), kslab)
    kslab = jnp.where(lane == ONES_LANE - 128, 1.0, kslab)
    k_ref[0, :, 0:128] = ckv.astype(BF16)
    k_ref[0, :, 128:256] = kslab.astype(BF16)
    q = _dot(cq.astype(BF16), wq_ref[...])
    hw = heads * 128
    for h in range(heads):
        qn = q[:, h * 128:(h + 1) * 128]
        ql = _dot(qn.astype(BF16), wuk_ref[h])
        qp = q[:, hw + h * 128:hw + (h + 1) * 128] * cos + q[:, 2 * hw + h * 128:2 * hw + (h + 1) * 128] * sin
        qp = jnp.where(lane == MASK_LANE - 128, 1.0, qp * qscale)
        q_ref[0, h, :, 0:128] = (ql * qscale).astype(BF16)
        q_ref[0, h, :, 128:256] = qp.astype(BF16)


def _flash_plan(n_tiles):
    big = FLASH_BIG if n_tiles > FLASH_BIG else 1
    n_big, last = divmod(n_tiles - 1, big)
    return big, n_big, last


def _flash_kernel(q_ref, k_ref, o_ref, *scratch, tq, heads, n_tiles, pad):
    all_pad = (pl.program_id(1) + 1) * tq <= pad

    @pl.when(all_pad)
    def _():
        o_ref[...] = jnp.zeros_like(o_ref)

    @pl.when(jnp.logical_not(all_pad))
    def _():
        _flash_block(q_ref, k_ref, o_ref, *scratch, tq=tq, heads=heads, n_tiles=n_tiles)


def _flash_block(q_ref, k_ref, o_ref, m_sc, acc_sc, sa_sc, sb_sc, mca_sc, mcb_sc, *, tq, heads, n_tiles):
    rows = heads * tq
    big, n_big, last = _flash_plan(n_tiles)
    n_chunks = 1 + n_big + (1 if last else 0)
    q = q_ref[0].reshape(rows, 256)
    m_sc[...] = jnp.full((rows, 128), -jnp.inf, F32)
    acc_sc[...] = jnp.zeros((rows, 256), F32)
    slots = ((sa_sc, mca_sc), (sb_sc, mcb_sc))

    def big_start(k):
        return 256 + (k - 1) * (big * 256)

    def chunk(k):
        if k == 0:
            return 0, 1
        if k <= n_big:
            return big_start(k), big
        return big_start(n_big + 1), last

    def scores(start, n_t, slot):
        s_ref, mc_ref = slot
        kc = k_ref[0, pl.ds(start, n_t * 256), :]
        mc = None
        for t in range(n_t):
            st = _dot_nt(q, kc[t * 256:(t + 1) * 256])
            s_ref[t] = st
            mt = jnp.maximum(st[:, 0:128], st[:, 128:256])
            mc = mt if mc is None else jnp.maximum(mc, mt)
        mc_ref[...] = mc

    def softmax_values(start, n_t, slot):
        s_ref, mc_ref = slot
        kc = k_ref[0, pl.ds(start, n_t * 256), :]
        m_old = m_sc[...]
        m_cur = jnp.max(mc_ref[...], axis=-1, keepdims=True)
        m_new = jnp.maximum(m_old, jnp.broadcast_to(m_cur, (rows, 128)))
        alpha = jnp.exp2(m_old - m_new)
        m2 = jnp.tile(m_new, (1, 2))
        p = jnp.concatenate([jnp.exp2((s_ref[t] - m2).astype(BF16)) for t in range(n_t)], axis=1)
        acc_sc[...] = jnp.tile(alpha, (1, 2)) * acc_sc[...] + _dot(p, kc)
        m_sc[...] = m_new

    def static_step(k):
        if k + 1 < n_chunks:
            scores(*chunk(k + 1), slots[(k + 1) % 2])
        softmax_values(*chunk(k), slots[k % 2])

    scores(*chunk(0), slots[0])
    static_step(0)
    n_loop = max(n_big - 1, 0) // FLASH_UNROLL

    def body(i, carry):
        k0 = 1 + FLASH_UNROLL * i
        for r in range(FLASH_UNROLL):
            nxt = pl.multiple_of(big_start(k0 + r + 1), 256)
            cur = pl.multiple_of(big_start(k0 + r), 256)
            scores(nxt, big, slots[r % 2])
            softmax_values(cur, big, slots[(r + 1) % 2])
        return carry

    lax.fori_loop(0, n_loop, body, 0)
    for k in range(1 + FLASH_UNROLL * n_loop, n_chunks):
        static_step(k)
    acc = acc_sc[...]
    out = acc[:, 0:128] / acc[:, ONES_LANE:ONES_LANE + 1]
    for h in range(heads):
        o_ref[0, :, h * 128:(h + 1) * 128] = out[h * tq:(h + 1) * tq].astype(o_ref.dtype)


def _mla_out_kernel(ol_ref, x_ref, wuv_ref, wo_ref, g_ref, b_ref, o_ref, *, heads, alpha):
    ol = ol_ref[0]
    parts = [_dot(ol[:, h * 128:(h + 1) * 128], wuv_ref[h]).astype(BF16) for h in range(heads)]
    o = jnp.concatenate(parts, axis=-1)
    m = _dot(o, wo_ref[...])
    o_ref[0] = _layer_norm(alpha * x_ref[0] + m, g_ref[...], b_ref[...])


def _mla_layer(h, cos128, sin128, w_in, q_norm, w_q_up, kv_norm, w_kv_up, w_o, lg, lb, pad, alpha):
    B, Tp, D = h.shape
    heads = w_kv_up.shape[1]
    hw = heads * 128

    def rot(w):
        return jnp.concatenate([-w[..., MLA_ROPE // 2:], w[..., :MLA_ROPE // 2]], axis=-1)

    def slab(w):
        return jnp.concatenate([w, jnp.zeros(w.shape[:-1] + (128 - MLA_ROPE,), w.dtype)], axis=-1)

    kpe_w = w_in[:, MLA_Q_RANK + MLA_KV_RANK:]
    w_in_aug = jnp.concatenate(
        [w_in[:, :MLA_Q_RANK + MLA_KV_RANK], slab(kpe_w), slab(rot(kpe_w))], axis=1).astype(BF16)
    wq3 = w_q_up.reshape(MLA_Q_RANK, heads, MLA_NOPE + MLA_ROPE)
    wq_pe = wq3[:, :, MLA_NOPE:]
    wq_all = jnp.concatenate(
        [wq3[:, :, :MLA_NOPE].reshape(MLA_Q_RANK, hw),
         slab(wq_pe).reshape(MLA_Q_RANK, hw),
         slab(rot(wq_pe)).reshape(MLA_Q_RANK, hw)], axis=1).astype(BF16)
    w_ukT = jnp.transpose(w_kv_up[:, :, :MLA_NOPE], (1, 2, 0)).astype(BF16)
    w_uv = jnp.transpose(w_kv_up[:, :, MLA_NOPE:], (1, 0, 2)).astype(BF16)
    qscale = (MLA_NOPE + MLA_ROPE) ** -0.5 * math.log2(math.e)

    tm = _row_tile(Tp, 256)
    n = Tp // tm
    q, k = pl.pallas_call(
        functools.partial(_mla_proj_kernel, tm=tm, pad=pad, heads=heads, qscale=qscale),
        grid=(B, n),
        in_specs=[_row_spec(tm, D), _row_spec(tm, 128), _row_spec(tm, 128),
                  _const_spec(w_in_aug.shape), _const_spec((1, MLA_Q_RANK)), _const_spec((1, MLA_KV_RANK)),
                  _const_spec(wq_all.shape), _const_spec(w_ukT.shape)],
        out_specs=[pl.BlockSpec((1, heads, tm, 256), lambda b, i: (b, 0, i, 0)), _row_spec(tm, 256)],
        out_shape=[jax.ShapeDtypeStruct((B, heads, Tp, 256), BF16),
                   jax.ShapeDtypeStruct((B, Tp, 256), BF16)],
        compiler_params=_params(),
        name="mla_proj",
    )(h, cos128, sin128, w_in_aug, q_norm.reshape(1, -1), kv_norm.reshape(1, -1), wq_all, w_ukT)

    tq = 64
    n_tiles = Tp // 256
    big = _flash_plan(n_tiles)[0]
    rows = heads * tq
    o_lat = pl.pallas_call(
        functools.partial(_flash_kernel, tq=tq, heads=heads, n_tiles=n_tiles, pad=pad),
        grid=(B, Tp // tq),
        in_specs=[pl.BlockSpec((1, heads, tq, 256), lambda b, i: (b, 0, i, 0)),
                  pl.BlockSpec((1, Tp, 256), lambda b, i: (b, 0, 0))],
        out_specs=_row_spec(tq, hw),
        out_shape=jax.ShapeDtypeStruct((B, Tp, hw), BF16),
        scratch_shapes=[pltpu.VMEM((rows, 128), F32), pltpu.VMEM((rows, 256), F32),
                        pltpu.VMEM((big, rows, 256), F32), pltpu.VMEM((big, rows, 256), F32),
                        pltpu.VMEM((rows, 128), F32), pltpu.VMEM((rows, 128), F32)],
        compiler_params=_params(),
        name="mla_flash",
    )(q, k)

    tm = _row_tile(Tp, 640)
    return pl.pallas_call(
        functools.partial(_mla_out_kernel, heads=heads, alpha=alpha),
        grid=(B, Tp // tm),
        in_specs=[_row_spec(tm, hw), _row_spec(tm, D), _const_spec(w_uv.shape),
                  _const_spec(w_o.shape), _const_spec((1, D)), _const_spec((1, D))],
        out_specs=_row_spec(tm, D),
        out_shape=jax.ShapeDtypeStruct((B, Tp, D), F32),
        compiler_params=_params(),
        name="mla_out",
    )(o_lat, h, w_uv, w_o.astype(BF16), lg.reshape(1, D), lb.reshape(1, D))


def _ffn_kernel(x_ref, xp_ref, xn_ref, wg_ref, wu_ref, cw_ref, cb_ref, wo_ref, g_ref, b_ref, o_ref,
                *, tm, pad, n_tiles, ff_chunk, alpha):
    i = pl.program_id(1)
    x = x_ref[0]
    xb = x.astype(BF16)
    ext = jnp.concatenate([xp_ref[0], x, xn_ref[0]], axis=0).astype(BF16)
    n_ext = tm + 2 * HALO
    row = i * tm - HALO + lax.broadcasted_iota(jnp.int32, (n_ext, 1), 0)
    ok = (row >= pad) & (row < n_tiles * tm)
    d_ff = wg_ref.shape[1]
    acc = jnp.zeros((tm, x.shape[1]), F32)
    for j in range(d_ff // ff_chunk):
        cs = slice(j * ff_chunk, (j + 1) * ff_chunk)
        g = jnp.where(ok, _dot(ext, wg_ref[:, cs]), 0.0)
        u = _dot(xb, wu_ref[:, cs])
        gc = (cw_ref[0:1, cs] * _shift(g, -1, tm) + cw_ref[1:2, cs] * _shift(g, 0, tm)
              + cw_ref[2:3, cs] * _shift(g, 1, tm) + cb_ref[:, cs])
        act = gc * _sigmoid(gc) * u
        acc = acc + _dot(act.astype(BF16), wo_ref[cs, :])
    o_ref[0] = _layer_norm(alpha * x + acc, g_ref[...], b_ref[...])


def _ffn_layer(h, w_in, conv_w, conv_b, w_out, lg, lb, pad, alpha):
    B, Tp, D = h.shape
    d_ff = w_out.shape[0]
    tm = _row_tile(Tp, 640)
    n = Tp // tm
    prev, nxt = _halo_specs(tm, D, Tp)
    wg = w_in[:, :d_ff].astype(BF16)
    wu = w_in[:, d_ff:].astype(BF16)
    return pl.pallas_call(
        functools.partial(_ffn_kernel, tm=tm, pad=pad, n_tiles=n, ff_chunk=d_ff, alpha=alpha),
        grid=(B, n),
        in_specs=[_row_spec(tm, D), prev, nxt, _const_spec(wg.shape), _const_spec(wu.shape),
                  _const_spec(conv_w.shape), _const_spec((1, d_ff)), _const_spec(w_out.shape),
                  _const_spec((1, D)), _const_spec((1, D))],
        out_specs=_row_spec(tm, D),
        out_shape=jax.ShapeDtypeStruct((B, Tp, D), F32),
        compiler_params=_params(),
        name="ffn",
    )(h, h, h, wg, wu, conv_w, conv_b.reshape(1, d_ff), w_out.astype(BF16),
      lg.reshape(1, D), lb.reshape(1, D))


def _seg_sum(z, e_ref):
    hi, lo = _split2(z)
    return _dot(hi, e_ref[...]) + _dot(lo, e_ref[...])


def _seg_bcast(zs, et_ref):
    hi, lo = _split2(zs)
    return _dot(hi, et_ref[...]) + _dot(lo, et_ref[...])


def _rw_proj_kernel(x_ref, xp_ref, xn_ref, mu_ref, wr_ref, wk_ref, wv_ref, g1_ref, g2_ref,
                    w1_ref, w2_ref, w0_ref, a1_ref, a2_ref, a0_ref, kk_ref, ka_ref, rk_ref,
                    e_ref, et_ref,
                    r_out, v_out, kk_out, g_out, bonus_out, lw0_out, lw1_out, b0_out, b1_out,
                    kd0_out, kd1_out, *, tm, pad, n_tiles):
    i = pl.program_id(1)
    ext = _ext_rows(x_ref[0], xp_ref[0], xn_ref[0], i, n_tiles, tm, pad)
    x = _shift(ext, 0, tm)
    xx = 0.5 * (_shift(ext, -1, tm) + _shift(ext, 1, tm)) - x
    mix = [(x + xx * mu_ref[c:c + 1, :]).astype(BF16) for c in range(6)]
    xr, xw, xk, xv, xa, xg = mix
    row = i * tm + lax.broadcasted_iota(jnp.int32, (tm, 1), 0)
    ok = row >= pad
    r = _dot(xr, wr_ref[...])
    k = jnp.where(ok, _dot(xk, wk_ref[...]), 0.0)
    v = jnp.where(ok, _dot(xv, wv_ref[...]), 0.0)
    g = _dot(_sigmoid(_dot(xg, g1_ref[...])).astype(BF16), g2_ref[...])
    tw = jnp.tanh(_dot(xw, w1_ref[...])).astype(BF16)
    ta = _dot(xa, a1_ref[...]).astype(BF16)
    kk = k * kk_ref[...]
    ss = _seg_sum(kk * kk, e_ref)
    kk = kk * _seg_bcast(lax.rsqrt(jnp.maximum(ss, 1e-24)), et_ref)
    r_out[0] = r.astype(r_out.dtype)
    v_out[0] = v.astype(v_out.dtype)
    kk_out[0] = kk.astype(kk_out.dtype)
    g_out[0] = g
    kd_sum = None
    for d, (lw_out, b_out, kd_out) in enumerate(((lw0_out, b0_out, kd0_out), (lw1_out, b1_out, kd1_out))):
        wl = w0_ref[d:d + 1, :] + _dot(tw[:, d * 128:(d + 1) * 128], w2_ref[d])
        w_log = -_softplus(-wl) - 0.5
        lw_out[0] = -jnp.exp(w_log)
        a = _sigmoid(a0_ref[d:d + 1, :] + _dot(ta[:, d * 128:(d + 1) * 128], a2_ref[d]))
        kd = k * (1.0 + (a - 1.0) * ka_ref[...])
        kd_out[0] = kd.astype(kd_out.dtype)
        b_out[0] = (kk * a).astype(b_out.dtype)
        kd_sum = kd if kd_sum is None else kd_sum + kd
    bs = _seg_sum(r * kd_sum * rk_ref[...], e_ref)
    bonus_out[0] = _seg_bcast(bs, et_ref) * v


def _rw_scan_kernel(r_ref, lw_ref, kk_ref, b_ref, kd_ref, v_ref, y_ref, h_sc, *, rev, n_sub):
    C = RW_CHUNK
    W = RW_GROUP * RW_HEAD
    groups = r_ref.shape[2] // W

    @pl.when(pl.program_id(1) == 0)
    def _():
        h_sc[...] = jnp.zeros_like(h_sc)

    ti = lax.broadcasted_iota(jnp.int32, (C, W), 0)
    si = lax.broadcasted_iota(jnp.int32, (C, W), 1) % RW_HEAD
    if rev:
        before = si > ti
    else:
        before = si < ti
    incl = before | (si == ti)
    eye = (si == ti).astype(F32)
    lvl_masks = []
    m = 1
    while m < C:
        same = (ti // (2 * m)) == (si // (2 * m))
        t_hi = (ti // m) % 2 == 1
        s_hi = (si // m) % 2 == 1
        if rev:
            lvl_masks.append(same & (~t_hi) & s_hi)
        else:
            lvl_masks.append(same & t_hi & (~s_hi))
        m *= 2
    bi = lax.broadcasted_iota(jnp.int32, (W, W), 0) // RW_HEAD
    bj = lax.broadcasted_iota(jnp.int32, (W, W), 1) // RW_HEAD
    bd_mask = bi == bj
    ones_bd = bd_mask.astype(BF16)
    ci =lax.broadcasted_iota(jnp.int32, (C, C), 0)
    cj = lax.broadcasted_iota(jnp.int32, (C, C), 1)
    tri = ((cj >= ci) if rev else (cj <= ci)).astype(BF16)

    def bd(z):
        zb = z.astype(BF16)
        return jnp.concatenate([zb] * RW_GROUP, axis=0) * ones_bd

    def fold(z):
        zm = jnp.where(bd_mask, z, 0.0)
        out = zm[0:RW_HEAD]
        for a in range(1, RW_GROUP):
            out = out + zm[a * RW_HEAD:(a + 1) * RW_HEAD]
        return out

    order = list(range(n_sub - 1, -1, -1)) if rev else list(range(n_sub))
    units = [(c, gi) for c in order for gi in range(groups)]

    def load(ref, u):
        c, gi = u
        return ref[0, c * C:(c + 1) * C, gi * W:(gi + 1) * W].astype(F32)

    def split3(z):
        z1 = z.astype(BF16)
        rem = z - z1.astype(F32)
        z2 = rem.astype(BF16)
        z3 = (rem - z2.astype(F32)).astype(BF16)
        return z1, z2, z3

    gsum, gtot = {}, {}
    for u in units:
        l1, l2, l3 = split3(load(lw_ref, u))
        gs = _dot(tri, l1) + _dot(tri, l2) + _dot(tri, l3)
        gsum[u] = gs
        gtot[u] = gs[0:1] if rev else gs[C - 1:C]
    lhs, Bt, Kt, lhs_t, gam = {}, {}, {}, {}, {}
    for u in units:
        gs = gsum[u]
        bb = load(b_ref, u)
        kd = load(kd_ref, u)
        en = jnp.exp(-gs)
        ec = jnp.exp(gtot[u] - gs)
        At = -load(kk_ref, u) * jnp.exp(gs - load(lw_ref, u))
        Rt = load(r_ref, u) * jnp.exp(gs)
        lhs[u] = jnp.concatenate([At, Rt], axis=0).astype(BF16)
        Bt[u] = bb * en
        Kt[u] = kd * en
        lhs_t[u] = jnp.concatenate([bb * ec, kd * ec], axis=0).astype(BF16)
        d1, d2, d3 = split3(eye * gtot[u])
        dd = _dot(jnp.concatenate([d1, d2, d3], axis=0), ones_bd)
        gam[u] = jnp.exp(dd[0:C] + dd[C:2 * C] + dd[2 * C:3 * C])
    A_ab, AA, A_rb = {}, {}, {}
    for u in units:
        sab = _dot_nt(lhs[u], bd(Bt[u]))
        sak = _dot_nt(lhs[u], bd(Kt[u]))
        A_ab[u] = sab[0:C]
        A_rb[u] = jnp.where(incl, sab[C:2 * C], 0.0).astype(BF16)
        AA[u] = jnp.concatenate([jnp.where(before, sak[0:C], 0.0),
                                 jnp.where(incl, sak[C:2 * C], 0.0)], axis=0).astype(BF16)
    X = {u: eye + jnp.where(lvl_masks[0], A_ab[u], 0.0) for u in units}
    for lm in lvl_masks[1:]:
        P = {u: _dot(X[u].astype(BF16), bd(jnp.where(lm, A_ab[u], 0.0))) for u in units}
        X = {u: X[u] + _dot(P[u].astype(BF16), bd(X[u])) for u in units}

    H = [h_sc[gi] for gi in range(groups)]
    for c in order:
        us = [(c, gi) for gi in range(groups)]
        vv = [load(v_ref, u) for u in us]
        AR = [_dot(lhs[u], bd(H[gi])) for gi, u in enumerate(us)]
        AV = [_dot(AA[u], bd(vv[gi])) for gi, u in enumerate(us)]
        U = [_dot(X[u].astype(BF16), bd(AR[gi][0:C] + AV[gi][0:C])) for gi, u in enumerate(us)]
        for gi, u in enumerate(us):
            y_ref[0, c * C:(c + 1) * C, gi * W:(gi + 1) * W] = (
                AR[gi][C:2 * C] + AV[gi][C:2 * C] + _dot(A_rb[u], bd(U[gi])))
        H = [gam[u] * H[gi]
             + fold(_dot_tn(lhs_t[u], jnp.concatenate([U[gi], vv[gi]], axis=0).astype(BF16)))
             for gi, u in enumerate(us)]
    for gi in range(groups):
        h_sc[gi] = H[gi]


def _rw_out_kernel(y0_ref, y1_ref, bonus_ref, g_ref, x_ref, gng_ref, gnb_ref, wo_ref, e_ref, et_ref,
                   lg_ref, lb_ref, o_ref, *, alpha):
    y = y0_ref[0] + y1_ref[0]
    inv_n = 1.0 / RW_HEAD
    mu = _seg_bcast(_seg_sum(y, e_ref) * inv_n, et_ref)
    yc = y - mu
    var = _seg_sum(yc * yc, e_ref) * inv_n
    yn = yc * _seg_bcast(lax.rsqrt(var + RW_GN_EPS), et_ref) * gng_ref[...] + gnb_ref[...]
    yo = (yn + bonus_ref[0]) * g_ref[0]
    m = _dot(yo.astype(BF16), wo_ref[...])
    o_ref[0] = _layer_norm(alpha * x_ref[0] + m, lg_ref[...], lb_ref[...])


def _rwkv_layer(h, mu, w_rkv, w0, w1, w2, a0, a1, a2, g1, g2, k_k, k_a, r_k, gn_g, gn_b, w_o,
                lg, lb, pad, alpha):
    B, Tp, D = h.shape
    heads = D // RW_HEAD

    def pad_cols(w, n):
        return jnp.concatenate([w, jnp.zeros(w.shape[:-1] + (n - w.shape[-1],), w.dtype)], axis=-1)

    def pad_rows(w, n):
        return jnp.concatenate([w, jnp.zeros(w.shape[:-2] + (n - w.shape[-2], w.shape[-1]), w.dtype)], axis=-2)

    g1p = pad_cols(g1, 256).astype(BF16)
    g2p = pad_rows(g2, 256).astype(BF16)
    w1p = jnp.concatenate([pad_cols(w1[0], 128), pad_cols(w1[1], 128)], axis=1).astype(BF16)
    a1p = jnp.concatenate([pad_cols(a1[0], 128), pad_cols(a1[1], 128)], axis=1).astype(BF16)
    w2p = pad_rows(w2, 128).astype(BF16)
    a2p = pad_rows(a2, 128).astype(BF16)
    e = (jnp.arange(D)[:, None] // RW_HEAD == jnp.arange(128)[None, :]).astype(BF16)
    et = e.T

    tm = _row_tile(Tp, 256)
    n = Tp // tm
    prev, nxt = _halo_specs(tm, D, Tp)
    vec = _const_spec((1, D))
    outs = pl.pallas_call(
        functools.partial(_rw_proj_kernel, tm=tm, pad=pad, n_tiles=n),
        grid=(B, n),
        in_specs=[_row_spec(tm, D), prev, nxt, _const_spec((6, D)),
                  _const_spec((D, D)), _const_spec((D, D)), _const_spec((D, D)),
                  _const_spec(g1p.shape), _const_spec(g2p.shape),
                  _const_spec(w1p.shape), _const_spec(w2p.shape), _const_spec((2, D)),
                  _const_spec(a1p.shape), _const_spec(a2p.shape), _const_spec((2, D)),
                  vec, vec, vec, _const_spec(e.shape), _const_spec(et.shape)],
        out_specs=[_row_spec(tm, D)] * 11,
        out_shape=[jax.ShapeDtypeStruct((B, Tp, D), dt) for dt in
                   (BF16, BF16, BF16, F32, F32, F32, F32, BF16, BF16, BF16, BF16)],
        compiler_params=_params(),
        name="rw_proj",
    )(h, h, h, mu, w_rkv[0].astype(BF16), w_rkv[1].astype(BF16), w_rkv[2].astype(BF16), g1p, g2p,
      w1p, w2p, w0, a1p, a2p, a0, k_k.reshape(1, D), k_a.reshape(1, D), r_k.reshape(1, D), e, et)
    r, v, kk, g, bonus, lw0, lw1, b0, b1, kd0, kd1 = outs

    ts = _row_tile(Tp, 256)
    ns = Tp // ts
    ys = []
    for rev, lw, bb, kd in ((False, lw0, b0, kd0), (True, lw1, b1, kd1)):
        if rev:
            spec = pl.BlockSpec((1, ts, D), lambda b, i: (b, ns - 1 - i, 0))
        else:
            spec = _row_spec(ts, D)
        ys.append(pl.pallas_call(
            functools.partial(_rw_scan_kernel, rev=rev, n_sub=ts // RW_CHUNK),
            grid=(B, ns),
            in_specs=[spec] * 6,
            out_specs=spec,
            out_shape=jax.ShapeDtypeStruct((B, Tp, D), F32),
            scratch_shapes=[pltpu.VMEM((D // (RW_GROUP * RW_HEAD), RW_HEAD, RW_GROUP * RW_HEAD), F32)],
            compiler_params=pltpu.CompilerParams(
                dimension_semantics=("parallel", "arbitrary"), vmem_limit_bytes=VMEM_LIMIT),
            name="rw_scan_bwd" if rev else "rw_scan_fwd",
        )(r, lw, kk, bb, kd, v))

    tm = _row_tile(Tp, 640)
    return pl.pallas_call(
        functools.partial(_rw_out_kernel, alpha=alpha),
        grid=(B, Tp // tm),
        in_specs=[_row_spec(tm, D)] * 5 + [vec, vec, _const_spec((D, D)), _const_spec(e.shape),
                                            _const_spec(et.shape), vec, vec],
        out_specs=_row_spec(tm, D),
        out_shape=jax.ShapeDtypeStruct((B, Tp, D), F32),
        compiler_params=_params(),
        name="rw_out",
    )(ys[0], ys[1], bonus, g, h, gn_g.reshape(1, D), gn_b.reshape(1, D), w_o.astype(BF16), e, et,
      lg.reshape(1, D), lb.reshape(1, D))


def _gelu_tanh(z):
    return 0.5 * z * (1.0 + jnp.tanh(math.sqrt(2.0 / math.pi) * (z + 0.044715 * z * z * z)))


def _lru_proj_kernel(x_ref, wg_ref, wu_ref, gate_out, u_out, *, tm, pad):
    i = pl.program_id(1)
    xb = x_ref[0].astype(BF16)
    gate_out[0] = _gelu_tanh(_dot(xb, wg_ref[...]))
    row = i * tm + lax.broadcasted_iota(jnp.int32, (tm, 1), 0)
    u_out[0] = jnp.where(row >= pad, _dot(xb, wu_ref[...]), 0.0)


def _lru_scan_kernel(u_ref, up_ref, un_ref, cw_ref, cb_ref, gw_ref, gb_ref, lam_ref, h_out,
                     a_sc, b_sc, h_sc, *, tm, pad, n_tiles, rev):
    step = pl.program_id(1)
    i = (n_tiles - 1 - step) if rev else step

    @pl.when(step == 0)
    def _():
        h_sc[...] = jnp.zeros_like(h_sc)

    ext = _ext_rows(u_ref[0], up_ref[0], un_ref[0], i, n_tiles, tm, 0)
    uc = cb_ref[...]
    for kt in range(cw_ref.shape[0]):
        uc = uc + cw_ref[kt:kt + 1, :] * _shift(ext, kt - 2, tm)
    width = uc.shape[1]
    nblk = width // LRU_BLOCK
    sp = _softplus(-lam_ref[...])
    row = i * tm + lax.broadcasted_iota(jnp.int32, (tm, 1), 0)
    ok = row >= pad
    for nb in range(nblk):
        cs = slice(nb * LRU_BLOCK, (nb + 1) * LRU_BLOCK)
        ub = uc[:, cs]
        ubb = ub.astype(BF16)
        rg = _sigmoid(_dot(ubb, gw_ref[0, nb]) + gb_ref[0:1, cs])
        ig = _sigmoid(_dot(ubb, gw_ref[1, nb]) + gb_ref[1:2, cs])
        a = jnp.exp(-LRU_C * rg * sp[:, cs])
        a_sc[:, cs] = a
        b_sc[:, cs] = jnp.where(ok, jnp.sqrt(1.0 - a * a) * (ig * ub), 0.0)

    def body(s, hcur):
        t = (tm - 1 - s) if rev else s
        hnew = a_sc[pl.ds(t, 1), :] * hcur + b_sc[pl.ds(t, 1), :]
        h_out[0, pl.ds(t, 1), :] = hnew
        return hnew

    h_sc[...] = lax.fori_loop(0, tm, body, h_sc[...], unroll=8)


def _lru_out_kernel(h0_ref, h1_ref, gate_ref, x_ref, wo_ref, lg_ref, lb_ref, o_ref, *, alpha):
    hh = (h0_ref[0] + h1_ref[0]) * gate_ref[0]
    m = _dot(hh.astype(BF16), wo_ref[...])
    o_ref[0] = _layer_norm(alpha * x_ref[0] + m, lg_ref[...], lb_ref[...])


def _lru_layer(h, w_in, conv_w, conv_b, gate_w, gate_b, lam, w_o, lg, lb, pad, alpha):
    B, Tp, D = h.shape
    width = w_o.shape[0]
    tm = _row_tile(Tp, 640)
    n = Tp // tm
    gate, u = pl.pallas_call(
        functools.partial(_lru_proj_kernel, tm=tm, pad=pad),
        grid=(B, n),
        in_specs=[_row_spec(tm, D), _const_spec((D, width)), _const_spec((D, width))],
        out_specs=[_row_spec(tm, width)] * 2,
        out_shape=[jax.ShapeDtypeStruct((B, Tp, width), F32)] * 2,
        compiler_params=_params(),
        name="lru_proj",
    )(h, w_in[:, :width].astype(BF16), w_in[:, width:].astype(BF16))

    ts = _row_tile(Tp, 256)
    ns = Tp // ts
    nb8 = ts // HALO
    last = Tp // HALO - 1
    hs = []
    for d, rev in ((0, False), (1, True)):
        if rev:
            cur = pl.BlockSpec((1, ts, width), lambda b, s: (b, ns - 1 - s, 0))
            prev = pl.BlockSpec((1, HALO, width), lambda b, s: (b, jnp.maximum((ns - 1 - s) * nb8 - 1, 0), 0))
            nxt = pl.BlockSpec((1, HALO, width), lambda b, s: (b, jnp.minimum((ns - s) * nb8, last), 0))
        else:
            cur = _row_spec(ts, width)
            prev, nxt = _halo_specs(ts, width, Tp)
        hs.append(pl.pallas_call(
            functools.partial(_lru_scan_kernel, tm=ts, pad=pad, n_tiles=ns, rev=rev),
            grid=(B, ns),
            in_specs=[cur, prev, nxt, _const_spec(conv_w.shape), _const_spec((1, width)),
                      _const_spec(gate_w.shape[1:]), _const_spec((2, width)), _const_spec((1, width))],
            out_specs=cur,
            out_shape=jax.ShapeDtypeStruct((B, Tp, width), F32),
            scratch_shapes=[pltpu.VMEM((ts, width), F32), pltpu.VMEM((ts, width), F32),
                            pltpu.VMEM((1, width), F32)],
            compiler_params=pltpu.CompilerParams(
                dimension_semantics=("parallel", "arbitrary"), vmem_limit_bytes=VMEM_LIMIT),
            name="lru_scan_bwd" if rev else "lru_scan_fwd",
        )(u, u, u, conv_w, conv_b.reshape(1, width), gate_w[d].astype(BF16), gate_b[d],
          lam[d].reshape(1, width)))

    return pl.pallas_call(
        functools.partial(_lru_out_kernel, alpha=alpha),
        grid=(B, n),
        in_specs=[_row_spec(tm, width)] * 3 + [_row_spec(tm, D), _const_spec((width, D)),
                                               _const_spec((1, D)), _const_spec((1, D))],
        out_specs=_row_spec(tm, D),
        out_shape=jax.ShapeDtypeStruct((B, Tp, D), F32),
        compiler_params=_params(),
        name="lru_out",
    )(hs[0], hs[1], gate, h, w_o.astype(BF16), lg.reshape(1, D), lb.reshape(1, D))


def kernel(x, positions, meta_tokens, ln_g, ln_b, ffn_w_in, ffn_conv_w, ffn_conv_b, ffn_w_out, mla_w_in, mla_q_norm, mla_w_q_up, mla_kv_norm, mla_w_kv_up, mla_w_o, rw_mu, rw_w_rkv, rw_w0, rw_w1, rw_w2, rw_a0, rw_a1, rw_a2, rw_g1, rw_g2, rw_k_k, rw_k_a, rw_r_k, rw_gn_g, rw_gn_b, rw_w_o, lru_w_in, lru_conv_w, lru_conv_b, lru_gate_w, lru_gate_b, lru_lambda, lru_w_o):
    B, S, D = x.shape
    depth = ln_g.shape[0]
    T = S + N_META
    Tp = -(-T // SEQ_ALIGN) * SEQ_ALIGN
    pad = Tp - T
    alpha = (2.0 * depth) ** 0.25
    dt = x.dtype
    h = jnp.concatenate([jnp.zeros((B, pad, D), dt),
                         jnp.broadcast_to(meta_tokens[None].astype(dt), (B, N_META, D)), x], axis=1)
    pos = jnp.concatenate(
        [jnp.zeros((B, pad), jnp.int32),
         jnp.broadcast_to(jnp.arange(N_META, dtype=jnp.int32)[None, :], (B, N_META)),
         positions + N_META], axis=1)
    inv_freq = ROPE_BASE ** (-jnp.arange(0, MLA_ROPE, 2, dtype=F32) / MLA_ROPE)
    ang = pos.astype(F32)[..., None] * inv_freq
    zeros = jnp.zeros((B, Tp, 128 - MLA_ROPE), F32)
    cos = jnp.cos(ang)
    sin = jnp.sin(ang)
    cos128 = jnp.concatenate([cos, cos, zeros], axis=-1)
    sin128 = jnp.concatenate([sin, sin, zeros], axis=-1)
    for i in range(depth):
        kind = i % 3
        j = i // 3
        lg, lb = ln_g[i, 0], ln_b[i, 0]
        if kind == 0:
            h = _mla_layer(h, cos128, sin128, mla_w_in[j], mla_q_norm[j], mla_w_q_up[j], mla_kv_norm[j],
                           mla_w_kv_up[j], mla_w_o[j], lg, lb, pad, alpha)
        elif kind == 1:
            h = _rwkv_layer(h, rw_mu[j], rw_w_rkv[j], rw_w0[j], rw_w1[j], rw_w2[j], rw_a0[j], rw_a1[j],
                            rw_a2[j], rw_g1[j], rw_g2[j], rw_k_k[j], rw_k_a[j], rw_r_k[j], rw_gn_g[j],
                            rw_gn_b[j], rw_w_o[j], lg, lb, pad, alpha)
        else:
            h = _lru_layer(h, lru_w_in[j], lru_conv_w[j], lru_conv_b[j], lru_gate_w[j], lru_gate_b[j],
                           lru_lambda[j], lru_w_o[j], lg, lb, pad, alpha)
        h = _ffn_layer(h, ffn_w_in[i], ffn_conv_w[i], ffn_conv_b[i], ffn_w_out[i],
                       ln_g[i, 1], ln_b[i, 1], pad, alpha)
    return h[:, pad + N_META:]
```
